```python
import functools
import jax, jax.numpy as jnp
from jax import lax
import numpy as np

D_MODEL = 1024
BATCH = 2
SEQ = 8192
DEPTH = 4
DEC_BATCH = 32
DEC_SEQ = 4
PAST_LEN = 8192
PAGE_SIZE = 128

ATT_HEADS = 8
HEAD_DIM = 64
ATT_W = ATT_HEADS * HEAD_DIM
MOBA_BLOCK = 256
MOBA_TOPK = 3
Q_CHUNK = 128
ROPE_THETA = 500000.0
ROPE_DIM = HEAD_DIM // 4
HG_HEADS = 8
HG_DK = 64
HG_DV = 64
HG_W_K = HG_HEADS * HG_DK
HG_W_V = HG_HEADS * HG_DV
HG_CHUNK = 64
LRU_W = 512
LRU_BLOCKS = 8
LRU_BD = LRU_W // LRU_BLOCKS
LRU_C = 8.0
CONV_W = 4
D_FF = 3 * D_MODEL
FFN_CONV_W = 3
N_BRANCH = 3
ALPHA = (2.0 * DEPTH) ** 0.25
BETA = (8.0 * DEPTH) ** -0.25
LN_EPS = 1e-5
RMS_EPS = 1e-6
SQRT_EPS = 1e-12
NEG_INF = -1e30
IN_SIZES = (ATT_W, ATT_W, ATT_W, HG_W_K, HG_W_K, HG_W_V, HG_W_V, LRU_W, LRU_W, N_BRANCH * D_MODEL)
D_IN = 3 * ATT_W + 2 * HG_W_K + 2 * HG_W_V + 2 * LRU_W + N_BRANCH * D_MODEL

kernel_name = 'hybrid_moba_hgrn2_rglru_decode_step'


def split_cols(a, sizes):
    out, off = [], 0
    for s in sizes:
        out.append(a[..., off:off + s])
        off += s
    return out


def layer_norm(x, g, b):
    xf = x.astype(jnp.float32)
    mu = jnp.mean(xf, axis=-1, keepdims=True)
    var = jnp.mean(jnp.square(xf - mu), axis=-1, keepdims=True)
    return ((xf - mu) * lax.rsqrt(var + LN_EPS) * g + b).astype(x.dtype)


def rope_partial(x, pos):
    half = ROPE_DIM // 2
    inv = ROPE_THETA ** (-jnp.arange(half, dtype=jnp.float32) / half)
    ang = pos.astype(jnp.float32)[:, None] * inv[None, :]
    cos = jnp.cos(ang)[None, :, None, :]
    sin = jnp.sin(ang)[None, :, None, :]
    xr = x[..., :ROPE_DIM].astype(jnp.float32)
    x1, x2 = xr[..., :half], xr[..., half:]
    rot = jnp.concatenate([x1 * cos - x2 * sin, x2 * cos + x1 * sin], axis=-1).astype(x.dtype)
    return jnp.concatenate([rot, x[..., ROPE_DIM:]], axis=-1)


def causal_dwconv(x, buf, w, b):
    width, t = w.shape[0], x.shape[1]
    xp = jnp.concatenate([buf.astype(x.dtype), x], axis=1)
    y = b
    for j in range(width):
        y = y + w[j] * xp[:, j:j + t]
    return y.astype(x.dtype), xp[:, t:]


def moba_core(q, qpos, kb, vb, kmean, n_past, k_sel, k_own, v_own, kpos_own):
    B, T, H, Dh = q.shape
    scale = Dh ** -0.5
    qf = q.astype(jnp.float32)
    s_own = jnp.einsum('bqhd,bthd->bhqt', qf, k_own.astype(jnp.float32)) * scale
    s_own = jnp.where(kpos_own[None, None, None, :] <= qpos[None, None, :, None], s_own, NEG_INF)
    if k_sel == 0:
        p = jax.nn.softmax(s_own, axis=-1)
        return jnp.einsum('bhqt,bthd->bqhd', p, v_own.astype(jnp.float32)).astype(q.dtype)
    n_blk = kmean.shape[1]
    gate = jnp.einsum('bqhd,bnhd->bhqn', qf, kmean)
    gate = jnp.where(jnp.arange(n_blk)[None, None, None, :] < n_past, gate, NEG_INF)
    _, idx = lax.top_k(gate, k_sel)
    valid = idx < n_past
    bi = jnp.arange(B)[:, None, None, None]
    hi = jnp.arange(H)[None, :, None, None]
    k_g = kb[bi, idx, :, hi]
    v_g = vb[bi, idx, :, hi]
    s_sel = jnp.einsum('bqhd,bhqjtd->bhqjt', qf, k_g.astype(jnp.float32)) * scale
    s_sel = jnp.where(valid[..., None], s_sel, NEG_INF).reshape(B, H, T, k_sel * MOBA_BLOCK)
    p = jax.nn.softmax(jnp.concatenate([s_sel, s_own], axis=-1), axis=-1)
    p_sel = p[..., :k_sel * MOBA_BLOCK].reshape(B, H, T, k_sel, MOBA_BLOCK)
    p_own = p[..., k_sel * MOBA_BLOCK:]
    o = (jnp.einsum('bhqjt,bhqjtd->bqhd', p_sel, v_g.astype(jnp.float32))
         + jnp.einsum('bhqt,bthd->bqhd', p_own, v_own.astype(jnp.float32)))
    return o.astype(q.dtype)


def moba_prompt(q, k, v):
    B, S, H, Dh = q.shape
    n_blk = -(-S // MOBA_BLOCK)
    pad = n_blk * MOBA_BLOCK - S
    kp = jnp.pad(k, ((0, 0), (0, pad), (0, 0), (0, 0)))
    vp = jnp.pad(v, ((0, 0), (0, pad), (0, 0), (0, 0)))
    kb = kp.reshape(B, n_blk, MOBA_BLOCK, H, Dh)
    vb = vp.reshape(B, n_blk, MOBA_BLOCK, H, Dh)
    kmean = jnp.mean(kb.astype(jnp.float32), axis=2)
    k_sel = min(MOBA_TOPK, n_blk)
    n_qc = S // Q_CHUNK
    qc = jnp.moveaxis(q.reshape(B, n_qc, Q_CHUNK, H, Dh), 1, 0)

    def one_chunk(args):
        c, qb = args
        start = c * Q_CHUNK
        blk = start // MOBA_BLOCK
        qpos = start + jnp.arange(Q_CHUNK)
        k_own = lax.dynamic_slice_in_dim(kp, blk * MOBA_BLOCK, MOBA_BLOCK, axis=1)
        v_own = lax.dynamic_slice_in_dim(vp, blk * MOBA_BLOCK, MOBA_BLOCK, axis=1)
        kpos = blk * MOBA_BLOCK + jnp.arange(MOBA_BLOCK)
        return moba_core(qb, qpos, kb, vb, kmean, blk, k_sel, k_own, v_own, kpos)

    o = lax.map(one_chunk, (jnp.arange(n_qc), qc))
    return jnp.moveaxis(o, 0, 1).reshape(B, S, H, Dh)


def moba_sample(q, k, v, k_past, v_past):
    B, T, H, Dh = q.shape
    P = k_past.shape[1]
    n_past = P // MOBA_BLOCK
    own0 = n_past * MOBA_BLOCK
    k_own = jnp.concatenate([k_past[:, own0:], k], axis=1)
    v_own = jnp.concatenate([v_past[:, own0:], v], axis=1)
    kpos = jnp.arange(own0, P + T)
    qpos = P + jnp.arange(T)
    kb = k_past[:, :own0].reshape(B, n_past, MOBA_BLOCK, H, Dh)
    vb = v_past[:, :own0].reshape(B, n_past, MOBA_BLOCK, H, Dh)
    kmean = jnp.mean(kb.astype(jnp.float32), axis=2)
    return moba_core(q, qpos, kb, vb, kmean, n_past, min(MOBA_TOPK, n_past), k_own, v_own, kpos)


def gla_chunked(q, k, v, logf, s0, chunk):
    B, T, H, DK = q.shape
    DV = v.shape[-1]
    n = T // chunk

    def split(a):
        return jnp.moveaxis(a.reshape(B, n, chunk, H, a.shape[-1]), 1, 0)

    tri = jnp.tril(jnp.ones((chunk, chunk), dtype=bool))[None, :, :, None, None]

    def step(S, inp):
        qc, kc, vc, lc = inp
        cum = jnp.cumsum(lc, axis=1)
        diff = cum[:, :, None] - cum[:, None, :]
        dec = jnp.where(tri, jnp.exp(jnp.where(tri, diff, 0.0)), 0.0)
        att = jnp.einsum('bthk,btshk,bshk->bhts', qc, dec, kc)
        o = (jnp.einsum('bhts,bshv->bthv', att, vc)
             + jnp.einsum('bthk,bhkv->bthv', qc * jnp.exp(cum), S))
        last = cum[:, -1]
        S = (jnp.exp(last)[..., None] * S
             + jnp.einsum('bshk,bshv->bhkv', kc * jnp.exp(last[:, None] - cum), vc))
        return S, o

    S, o = lax.scan(step, s0, (split(q), split(k), split(v), split(logf)))
    return jnp.moveaxis(o, 0, 1).reshape(B, T, H, DV), S


def hgrn2_mix(q, f_raw, i, og, lb, gain, s0, chunk):
    B, T, _ = q.shape
    f_raw = f_raw.astype(jnp.float32)
    f = lb + (1.0 - lb) * jax.nn.sigmoid(f_raw)
    logf = jnp.log(f)
    kk = (1.0 - lb) * jax.nn.sigmoid(-f_raw)

    def heads(a, d):
        return a.astype(jnp.float32).reshape(B, T, HG_HEADS, d)

    o, s = gla_chunked(heads(q, HG_DK), heads(kk, HG_DK), heads(i, HG_DV), heads(logf, HG_DK),
                       s0.astype(jnp.float32), chunk)
    o = o * lax.rsqrt(jnp.mean(o * o, axis=-1, keepdims=True) + RMS_EPS)
    o = o.reshape(B, T, HG_W_V) * gain * jax.nn.silu(og.astype(jnp.float32))
    return o.astype(q.dtype), s


def linear_combine(left, right):
    a1, b1 = left
    a2, b2 = right
    return a1 * a2, a2 * b1 + b2


def rglru_mix(xb, gb, h0, buf, conv_w, conv_b, wa, ba, wx, bx, lam):
    xc, new_buf = causal_dwconv(xb, buf, conv_w, conv_b)
    B, T, W = xc.shape
    xf = xc.astype(jnp.float32)
    xh = xf.reshape(B, T, LRU_BLOCKS, LRU_BD)
    r = jax.nn.sigmoid(jnp.einsum('btnc,ncd->btnd', xh, wa.astype(jnp.float32)).reshape(B, T, W) + ba)
    ig = jax.nn.sigmoid(jnp.einsum('btnc,ncd->btnd', xh, wx.astype(jnp.float32)).reshape(B, T, W) + bx)
    log_a = -LRU_C * r * jax.nn.softplus(-lam.astype(jnp.float32))
    a = jnp.exp(log_a)
    u = jnp.sqrt(jnp.maximum(-jnp.expm1(2.0 * log_a), SQRT_EPS)) * ig * xf
    u = u.at[:, 0].add(a[:, 0] * h0.astype(jnp.float32))
    _, h = lax.associative_scan(linear_combine, (a, u), axis=1)
    y = h * jax.nn.gelu(gb.astype(jnp.float32))
    return y.astype(xb.dtype), h[:, -1], new_buf


def conv_ffn(x, buf, w_up, conv_w, conv_b, w_down):
    h = x @ w_up
    u, val = h[..., :D_FF], h[..., D_FF:]
    uc, new_buf = causal_dwconv(u, buf, conv_w, conv_b)
    return (jax.nn.gelu(uc) * val) @ w_down, new_buf


def mixer_block(x, pos, attn_fn, hg_s0, hg_chunk, lru_h0, lru_buf, mix_w):
    (w_in, lb, hg_gain, lru_conv_w, lru_conv_b, lru_wa, lru_ba, lru_wx, lru_bx,
     lru_lambda, w_br_att, w_br_hg, w_br_lru, w_o) = mix_w
    B, T, _ = x.shape
    qa, ka, va, qh, fh, ih, gh, xl, gl, gm = split_cols(x @ w_in, IN_SIZES)
    qa = rope_partial(qa.reshape(B, T, ATT_HEADS, HEAD_DIM), pos)
    ka = rope_partial(ka.reshape(B, T, ATT_HEADS, HEAD_DIM), pos)
    va = va.reshape(B, T, ATT_HEADS, HEAD_DIM)
    o_att = attn_fn(qa, ka, va).reshape(B, T, ATT_W)
    o_hg, hg_s = hgrn2_mix(qh, fh, ih, gh, lb, hg_gain, hg_s0, hg_chunk)
    o_lru, lru_h, lru_buf = rglru_mix(xl, gl, lru_h0, lru_buf, lru_conv_w, lru_conv_b,
                                      lru_wa, lru_ba, lru_wx, lru_bx, lru_lambda)
    g = jax.nn.sigmoid(gm.reshape(B, T, N_BRANCH, D_MODEL))
    merged = (g[:, :, 0] * (o_att @ w_br_att)
              + g[:, :, 1] * (o_hg @ w_br_hg)
              + g[:, :, 2] * (o_lru @ w_br_lru))
    return merged @ w_o, ka, va, hg_s, lru_h, lru_buf


def trunk_layer(x, pos, attn_fn, hg_s0, hg_chunk, lru_h0, lru_buf, ffn_buf, mix_w, ffn_w, ln_w):
    ln1_g, ln1_b, ln2_g, ln2_b = ln_w
    mix, k, v, hg_s, lru_h, lru_buf = mixer_block(x, pos, attn_fn, hg_s0, hg_chunk, lru_h0, lru_buf, mix_w)
    x = layer_norm(ALPHA * x + mix, ln1_g, ln1_b)
    f, ffn_buf = conv_ffn(x, ffn_buf, *ffn_w)
    x = layer_norm(ALPHA * x + f, ln2_g, ln2_b)
    return x, k, v, hg_s, lru_h, lru_buf, ffn_buf


def setup_inputs(seed: int = 0) -> dict:
    key = jax.random.key(seed)
    ks = jax.random.split(key, 32)
    n_pages = PAST_LEN // PAGE_SIZE
    n_pool = (5 * DEC_BATCH * n_pages + 3) // 4

    def nrm(k, shape, scale=1.0):
        return jax.random.normal(k, shape, jnp.float32) * scale

    a0 = jax.random.uniform(ks[14], (DEPTH, LRU_W), jnp.float32, 0.9, 0.999)
    s = a0 ** (1.0 / LRU_C)
    page_table = jax.random.permutation(ks[4], n_pool)[:DEC_BATCH * n_pages]
    return {
        'x_prompt': nrm(ks[0], (BATCH, SEQ, D_MODEL)),
        'x_sample': nrm(ks[1], (DEC_BATCH, DEC_SEQ, D_MODEL)),
        'cache_k': nrm(ks[2], (n_pool, DEPTH, PAGE_SIZE, ATT_HEADS, HEAD_DIM)),
        'cache_v': nrm(ks[3], (n_pool, DEPTH, PAGE_SIZE, ATT_HEADS, HEAD_DIM)),
        'page_table': page_table.reshape(DEC_BATCH, n_pages).astype(jnp.int32),
        'state_hgrn': nrm(ks[5], (DEC_BATCH, DEPTH, HG_HEADS, HG_DK, HG_DV), 0.5),
        'state_lru_h': nrm(ks[6], (DEC_BATCH, DEPTH, LRU_W), 0.5),
        'state_lru_conv': nrm(ks[7], (DEC_BATCH, DEPTH, CONV_W - 1, LRU_W)),
        'state_ffn_conv': nrm(ks[8], (DEC_BATCH, DEPTH, FFN_CONV_W - 1, D_FF)),
        'w_in': nrm(ks[9], (DEPTH, D_MODEL, D_IN), D_MODEL ** -0.5),
        'hg_lb_logits': nrm(ks[10], (DEPTH, HG_W_K), 0.5),
        'hg_gain': 1.0 + nrm(ks[11], (DEPTH, HG_W_V), 0.02),
        'lru_conv_w': nrm(ks[12], (DEPTH, CONV_W, LRU_W), CONV_W ** -0.5),
        'lru_conv_b': nrm(ks[13], (DEPTH, LRU_W), 0.02),
        'lru_wa': nrm(ks[15], (DEPTH, LRU_BLOCKS, LRU_BD, LRU_BD), LRU_BD ** -0.5),
        'lru_ba': nrm(ks[16], (DEPTH, LRU_W), 0.02),
        'lru_wx': nrm(ks[17], (DEPTH, LRU_BLOCKS, LRU_BD, LRU_BD), LRU_BD ** -0.5),
        'lru_bx': nrm(ks[18], (DEPTH, LRU_W), 0.02),
        'lru_lambda': jnp.log(s) - jnp.log1p(-s),
        'w_br_att': nrm(ks[19], (DEPTH, ATT_W, D_MODEL), ATT_W ** -0.5),
        'w_br_hg': nrm(ks[20], (DEPTH, HG_W_V, D_MODEL), HG_W_V ** -0.5),
        'w_br_lru': nrm(ks[21], (DEPTH, LRU_W, D_MODEL), LRU_W ** -0.5),
        'w_o': nrm(ks[22], (DEPTH, D_MODEL, D_MODEL), BETA * D_MODEL ** -0.5),
        'ln1_g': 1.0 + nrm(ks[23], (DEPTH, D_MODEL), 0.02),
        'ln1_b': nrm(ks[24], (DEPTH, D_MODEL), 0.02),
        'ffn_w_up': nrm(ks[25], (DEPTH, D_MODEL, 2 * D_FF), D_MODEL ** -0.5),
        'ffn_conv_w': nrm(ks[26], (DEPTH, FFN_CONV_W, D_FF), FFN_CONV_W ** -0.5),
        'ffn_conv_b': nrm(ks[27], (DEPTH, D_FF), 0.02),
        'ffn_w_down': nrm(ks[28], (DEPTH, D_FF, D_MODEL), BETA * D_FF ** -0.5),
        'ln2_g': 1.0 + nrm(ks[29], (DEPTH, D_MODEL), 0.02),
        'ln2_b': nrm(ks[30], (DEPTH, D_MODEL), 0.02),
    }


def reference(x_prompt, x_sample, cache_k, cache_v, page_table, state_hgrn, state_lru_h,
              state_lru_conv, state_ffn_conv, w_in, hg_lb_logits, hg_gain, lru_conv_w, lru_conv_b,
              lru_wa, lru_ba, lru_wx, lru_bx, lru_lambda, w_br_att, w_br_hg, w_br_lru, w_o,
              ln1_g, ln1_b, ffn_w_up, ffn_conv_w, ffn_conv_b, ffn_w_down, ln2_g, ln2_b):
    B, S, _ = x_prompt.shape
    DB, T, _ = x_sample.shape
    past = page_table.shape[1] * cache_k.shape[2]
    pos_p = jnp.arange(S)
    pos_s = past + jnp.arange(T)
    p_layer = jax.nn.softmax(hg_lb_logits.astype(jnp.float32), axis=0)
    lb_all = jnp.cumsum(p_layer, axis=0) - p_layer[:1]
    zero_hg = jnp.zeros((B, HG_HEADS, HG_DK, HG_DV), jnp.float32)
    zero_h = jnp.zeros((B, LRU_W), jnp.float32)
    zero_lbuf = jnp.zeros((B, CONV_W - 1, LRU_W), x_prompt.dtype)
    zero_fbuf = jnp.zeros((B, FFN_CONV_W - 1, D_FF), x_prompt.dtype)
    xp, xs = x_prompt, x_sample
    outs_p = [[] for _ in range(6)]
    outs_s = [[] for _ in range(6)]
    for l in range(DEPTH):
        mix_w = (w_in[l], lb_all[l], hg_gain[l], lru_conv_w[l], lru_conv_b[l], lru_wa[l], lru_ba[l],
                 lru_wx[l], lru_bx[l], lru_lambda[l], w_br_att[l], w_br_hg[l], w_br_lru[l], w_o[l])
        ffn_w = (ffn_w_up[l], ffn_conv_w[l], ffn_conv_b[l], ffn_w_down[l])
        ln_w = (ln1_g[l], ln1_b[l], ln2_g[l], ln2_b[l])
        xp, *new_p = trunk_layer(xp, pos_p, moba_prompt, zero_hg, min(HG_CHUNK, S), zero_h,
                                 zero_lbuf, zero_fbuf, mix_w, ffn_w, ln_w)
        k_past = cache_k[page_table, l].reshape(DB, past, ATT_HEADS, HEAD_DIM)
        v_past = cache_v[page_table, l].reshape(DB, past, ATT_HEADS, HEAD_DIM)
        attn_s = functools.partial(moba_sample, k_past=k_past, v_past=v_past)
        xs, *new_s = trunk_layer(xs, pos_s, attn_s, state_hgrn[:, l], T, state_lru_h[:, l],
                                 state_lru_conv[:, l], state_ffn_conv[:, l], mix_w, ffn_w, ln_w)
        for lst, a in zip(outs_p, new_p):
            lst.append(a)
        for lst, a in zip(outs_s, new_s):
            lst.append(a)
    k_p = jnp.stack(outs_p[0], axis=1)
    v_p = jnp.stack(outs_p[1], axis=1)
    k_s = jnp.stack(outs_s[0], axis=1)
    v_s = jnp.stack(outs_s[1], axis=1)
    hg_p = jnp.stack(outs_p[2], axis=1)
    hg_s = jnp.stack(outs_s[2], axis=1)
    lh_p = jnp.stack(outs_p[3], axis=1)
    lh_s = jnp.stack(outs_s[3], axis=1)
    lc_p = jnp.stack(outs_p[4], axis=1)
    lc_s = jnp.stack(outs_s[4], axis=1)
    fc_p = jnp.stack(outs_p[5], axis=1)
    fc_s = jnp.stack(outs_s[5], axis=1)
    return (xp, xs, k_p, v_p, k_s, v_s, hg_p, hg_s, lh_p, lh_s, lc_p, lc_s, fc_p, fc_s)
```

```python
import functools
import math

import jax
import jax.numpy as jnp
from jax import lax
from jax.experimental import pallas as pl
from jax.experimental.pallas import tpu as pltpu

F32 = jnp.float32
BF16 = jnp.bfloat16

D_MODEL = 1024
ATT_HEADS = 8
HEAD_DIM = 64
ATT_W = ATT_HEADS * HEAD_DIM
MOBA_BLOCK = 256
MOBA_TOPK = 3
ROPE_THETA = 500000.0
ROPE_DIM = HEAD_DIM // 4
HG_HEADS = 8
HG_DK = 64
HG_W = HG_HEADS * HG_DK
LRU_W = 512
LRU_BLOCKS = 8
LRU_BD = LRU_W // LRU_BLOCKS
LRU_C = 8.0
CONV_W = 4
D_FF = 3 * D_MODEL
FFN_CONV_W = 3
N_BRANCH = 3
LN_EPS = 1e-5
RMS_EPS = 1e-6
SQRT_EPS = 1e-12
NEG_INF = -1e30
D_IN = 3 * ATT_W + 4 * HG_W + 2 * LRU_W + N_BRANCH * D_MODEL

COLW = 512
C_QA, C_KA, C_VA, C_QH, C_FH, C_IH, C_GH, C_XL, C_GL, C_GM = 0, 1, 2, 3, 4, 5, 6, 7, 8, 9

LANES = 128
SUBLANES = 8
MXU_DIM = 256
VMEM_LIMIT = 56 * 1024 * 1024

HIGHEST = lax.Precision.HIGHEST
CONTRACT_LAST = (((1,), (1,)), ((), ()))
CONTRACT_FIRST = (((0,), (0,)), ((), ()))


def _params(n_grid):
    return pltpu.CompilerParams(dimension_semantics=("arbitrary",) * n_grid,
                                vmem_limit_bytes=VMEM_LIMIT)


def _sigmoid(x):
    return 1.0 / (1.0 + jnp.exp(-x))


def _gelu(x):
    c = math.sqrt(2.0 / math.pi)
    return x * (0.5 * (1.0 + jnp.tanh(c * (x + 0.044715 * (x * x * x)))))


def _layer_norm(y, g, b):
    mu = jnp.mean(y, axis=-1, keepdims=True)
    d = y - mu
    var = jnp.mean(d * d, axis=-1, keepdims=True)
    return d * lax.rsqrt(var + LN_EPS) * g + b


def _top_k_mask(gate, valid):
    lane = lax.broadcasted_iota(jnp.int32, gate.shape, 1).astype(F32)
    g = jnp.where(valid, gate, -jnp.inf)
    sel = jnp.zeros(gate.shape, F32)
    for _ in range(MOBA_TOPK):
        mx = jnp.max(g, axis=1, keepdims=True)
        idx = jnp.min(jnp.where(g == mx, lane, float(LANES)), axis=1, keepdims=True)
        pick = lane == idx
        sel = jnp.where(pick, 1.0, sel)
        g = jnp.where(pick, -jnp.inf, g)
    return jnp.where(valid, sel, 0.0)


def _mm_kernel(x_ref, w_ref, o_ref):
    o_ref[...] = jnp.dot(x_ref[...].astype(BF16), w_ref[...], preferred_element_type=F32)


def _matmul(x, w, bm, bn):
    m, k = x.shape
    n = w.shape[1]
    return pl.pallas_call(
        _mm_kernel,
        grid=(m // bm, n // bn),
        in_specs=[pl.BlockSpec((bm, k), lambda i, j: (i, 0)),
                  pl.BlockSpec((k, bn), lambda i, j: (0, j))],
        out_specs=pl.BlockSpec((bm, bn), lambda i, j: (i, j)),
        out_shape=jax.ShapeDtypeStruct((m, n), F32),
        compiler_params=_params(2),
        name="in_proj",
    )(x, w)


def _rope_kernel(q_ref, k_ref, v_ref, c_ref, sa_ref, sb_ref,
                 qo_ref, ko_ref, vo_ref, kb_ref, vb_ref, *ks_refs, tm, n_blk):
    c = c_ref[...]
    sa = sa_ref[...]
    sb = sb_ref[...]

    def rot(x):
        outs = []
        for j in range(ATT_W // LANES):
            xs = x[:, LANES * j:LANES * (j + 1)]
            outs.append(xs * c + pltpu.roll(xs, ROPE_DIM // 2, 1) * sa
                        + pltpu.roll(xs, LANES - ROPE_DIM // 2, 1) * sb)
        return jnp.concatenate(outs, axis=1)

    q = rot(q_ref[...])
    k = rot(k_ref[...])
    v = v_ref[...]
    qo_ref[...] = q
    ko_ref[...] = k
    vo_ref[...] = v
    kb_ref[...] = k.astype(BF16)
    vb_ref[...] = v.astype(BF16)
    if n_blk:
        ks_ref = ks_refs[0]
        for r in range(n_blk):
            ks_ref[r] = jnp.sum(k[MOBA_BLOCK * r:MOBA_BLOCK * (r + 1)], axis=0, keepdims=True)


def _rope_call(h3, tabs, tm, with_sums):
    b, s, _ = h3.shape
    nt = s // tm
    n_blk = tm // MOBA_BLOCK if with_sums else 0
    col = lambda cidx: pl.BlockSpec((None, tm, COLW), lambda bi, ti, cidx=cidx: (bi, ti, cidx))
    tab = pl.BlockSpec((tm, LANES), lambda bi, ti: (ti, 0))
    row = pl.BlockSpec((None, tm, ATT_W), lambda bi, ti: (bi, ti, 0))
    out_specs = [row, row, row, row, row]
    out_shape = [jax.ShapeDtypeStruct((b, s, ATT_W), F32)] * 3 + [jax.ShapeDtypeStruct((b, s, ATT_W), BF16)] * 2
    if with_sums:
        out_specs.append(pl.BlockSpec((None, n_blk, 1, ATT_W), lambda bi, ti: (bi, ti, 0, 0)))
        out_shape.append(jax.ShapeDtypeStruct((b, s // MOBA_BLOCK, 1, ATT_W), F32))
    return pl.pallas_call(
        functools.partial(_rope_kernel, tm=tm, n_blk=n_blk),
        grid=(b, nt),
        in_specs=[col(C_QA), col(C_KA), col(C_VA), tab, tab, tab],
        out_specs=out_specs,
        out_shape=out_shape,
        compiler_params=_params(2),
        name="rope",
    )(h3, h3, h3, *tabs)


def _rope_tables(pos):
    half = ROPE_DIM // 2
    inv = ROPE_THETA ** (-jnp.arange(half, dtype=F32) / half)
    ang = pos.astype(F32)[:, None] * inv[None, :]
    cos, sin = jnp.cos(ang), jnp.sin(ang)
    lh = jnp.arange(LANES) % HEAD_DIM
    fi = lh % half
    c = jnp.where(lh[None, :] < ROPE_DIM, cos[:, fi], 1.0)
    sa = jnp.where((lh[None, :] >= half) & (lh[None, :] < ROPE_DIM), sin[:, fi], 0.0)
    sb = jnp.where(lh[None, :] < half, -sin[:, fi], 0.0)
    return c.astype(F32), sa.astype(F32), sb.astype(F32)


def _attn_prompt_kernel(q_ref, k_ref, v_ref, ks_ref, o_ref, *, tq):
    i = pl.program_id(2)
    scale = HEAD_DIM ** -0.5
    q = q_ref[...]
    kmean = ks_ref[...] * (1.0 / MOBA_BLOCK)
    lane = lax.broadcasted_iota(jnp.int32, (tq, LANES), 1)
    row = lax.broadcasted_iota(jnp.int32, (tq, tq), 0)
    colk = lax.broadcasted_iota(jnp.int32, (tq, tq), 1)
    causal = colk <= row
    valid = lane < i
    own0 = pl.multiple_of(i * tq, tq)
    k_own = k_ref[pl.ds(own0, tq), :]
    v_own = v_ref[pl.ds(own0, tq), :]

    q_aug = []
    state = []
    for hh in range(2):
        in_head = (lane >= hh * HEAD_DIM) & (lane < (hh + 1) * HEAD_DIM)
        qh = jnp.where(in_head, q, 0.0)
        gate = lax.dot_general(qh, kmean, CONTRACT_LAST, precision=HIGHEST, preferred_element_type=F32)
        sel = _top_k_mask(gate, valid)
        bias = jnp.where(sel > 0.0, 0.0, NEG_INF)
        qs = (qh * scale).astype(BF16)
        q_aug.append(jnp.concatenate([qs, bias.astype(BF16)], axis=1))
        s = lax.dot_general(qs, k_own, CONTRACT_LAST, preferred_element_type=F32)
        s = jnp.where(causal, s, NEG_INF)
        m = jnp.max(s, axis=1, keepdims=True)
        p = jnp.exp(s - m)
        l = jnp.sum(p, axis=1, keepdims=True)
        acc = jnp.dot(p.astype(BF16), v_own, preferred_element_type=F32)
        state += [m, l, acc]

    def body(j, st):
        r0 = pl.multiple_of(j * tq, tq)
        kj = k_ref[pl.ds(r0, tq), :]
        vj = v_ref[pl.ds(r0, tq), :]
        onehot = jnp.where(lane == j, 1.0, 0.0).astype(BF16)
        k_aug = jnp.concatenate([kj, onehot], axis=1)
        out = []
        for hh in range(2):
            m, l, acc = st[3 * hh:3 * hh + 3]
            s = lax.dot_general(q_aug[hh], k_aug, CONTRACT_LAST, preferred_element_type=F32)
            m_new = jnp.maximum(m, jnp.max(s, axis=1, keepdims=True))
            alpha = jnp.exp(m - m_new)
            p = jnp.exp(s - m_new)
            l = alpha * l + jnp.sum(p, axis=1, keepdims=True)
            acc = alpha * acc + jnp.dot(p.astype(BF16), vj, preferred_element_type=F32)
            out += [m_new, l, acc]
        return tuple(out)

    st = lax.fori_loop(0, i, body, tuple(state))
    o0 = st[2] / st[1]
    o1 = st[5] / st[4]
    o_ref[...] = jnp.where(lane < HEAD_DIM, o0, o1)


def _attn_prompt_call(q, kb, vb, ksum):
    b, s, _ = q.shape
    tq = MOBA_BLOCK
    n_pairs = ATT_W // LANES
    return pl.pallas_call(
        functools.partial(_attn_prompt_kernel, tq=tq),
        grid=(b, n_pairs, s // tq),
        in_specs=[pl.BlockSpec((None, tq, LANES), lambda bi, hp, i: (bi, i, hp)),
                  pl.BlockSpec((None, s, LANES), lambda bi, hp, i: (bi, 0, hp)),
                  pl.BlockSpec((None, s, LANES), lambda bi, hp, i: (bi, 0, hp)),
                  pl.BlockSpec((None, LANES, LANES), lambda bi, hp, i: (bi, 0, hp))],
        out_specs=pl.BlockSpec((None, tq, LANES), lambda bi, hp, i: (bi, i, hp)),
        out_shape=jax.ShapeDtypeStruct((b, s, ATT_W), F32),
        compiler_params=_params(3),
        name="attn_prompt",
    )(q, kb, vb, ksum)


def _attn_sample_kernel(pt_ref, q_ref, kn_ref, vn_ref, ka_ref, kb_ref, va_ref, vb_ref, o_ref,
                        ksum_ref, m_ref, l_ref, oall_ref, *, n_tok, n_blocks, tpad):
    del pt_ref
    n = pl.program_id(1)
    rows = n_tok * ATT_HEADS
    scale = HEAD_DIM ** -0.5
    q = q_ref[...]
    qrep = jnp.concatenate([jnp.broadcast_to(q[t:t + 1], (ATT_HEADS, ATT_W)) for t in range(n_tok)], axis=0)
    rh = lax.broadcasted_iota(jnp.int32, (rows, ATT_W), 0) % ATT_HEADS
    lh = lax.broadcasted_iota(jnp.int32, (rows, ATT_W), 1) // HEAD_DIM
    head_mask = rh == lh
    qbd = jnp.where(head_mask, qrep, 0.0)
    qs = (qbd * scale).astype(BF16)
    lane = lax.broadcasted_iota(jnp.int32, (rows, LANES), 1)

    @pl.when(n == 0)
    def _():
        ksum_ref[...] = jnp.zeros(ksum_ref.shape, F32)
        m_ref[...] = jnp.full(m_ref.shape, NEG_INF, F32)
        l_ref[...] = jnp.zeros(l_ref.shape, F32)

    kblk = jnp.concatenate([ka_ref[...], kb_ref[...]], axis=0)
    vblk = jnp.concatenate([va_ref[...], vb_ref[...]], axis=0).astype(BF16)
    ksum_ref[pl.ds(n, 1), :] = jnp.sum(kblk, axis=0, keepdims=True)
    s = lax.dot_general(qs, kblk.astype(BF16), CONTRACT_LAST, preferred_element_type=F32)
    m_n = jnp.max(s, axis=1, keepdims=True)
    p = jnp.exp(s - m_n)
    l_n = jnp.sum(p, axis=1, keepdims=True)
    oall_ref[n] = jnp.dot(p.astype(BF16), vblk, preferred_element_type=F32)
    m_ref[...] = jnp.where(lane == n, m_n, m_ref[...])
    l_ref[...] = jnp.where(lane == n, l_n, l_ref[...])

    @pl.when(n == n_blocks - 1)
    def _():
        kmean = ksum_ref[...] * (1.0 / MOBA_BLOCK)
        gate = lax.dot_general(qbd, kmean, CONTRACT_LAST, precision=HIGHEST, preferred_element_type=F32)
        valid = lane < n_blocks
        sel = _top_k_mask(gate, valid) > 0.0
        m_all = m_ref[...]
        l_all = l_ref[...]
        kn = kn_ref[...].astype(BF16)
        vn = vn_ref[...].astype(BF16)
        s_own = lax.dot_general(qs, kn, CONTRACT_LAST, preferred_element_type=F32)
        tq = lax.broadcasted_iota(jnp.int32, (rows, tpad), 0) // ATT_HEADS
        tk = lax.broadcasted_iota(jnp.int32, (rows, tpad), 1)
        s_own = jnp.where((tk <= tq) & (tk < n_tok), s_own, NEG_INF)
        m_own = jnp.max(s_own, axis=1, keepdims=True)
        m_fin = jnp.maximum(m_own, jnp.max(jnp.where(sel, m_all, NEG_INF), axis=1, keepdims=True))
        p_own = jnp.exp(s_own - m_fin)
        w = jnp.where(sel, jnp.exp(m_all - m_fin), 0.0)
        l_fin = jnp.sum(p_own, axis=1, keepdims=True) + jnp.sum(w * l_all, axis=1, keepdims=True)
        o = jnp.dot(p_own.astype(BF16), vn, preferred_element_type=F32)
        for nb in range(n_blocks):
            o = o + w[:, nb:nb + 1] * oall_ref[nb]
        o = jnp.where(head_mask, o / l_fin, 0.0)
        o_tok = jnp.sum(o.reshape(n_tok, ATT_HEADS, ATT_W), axis=1)
        o_ref[...] = jnp.concatenate([o_tok, jnp.zeros((tpad - n_tok, ATT_W), F32)], axis=0)


def _attn_sample_call(page_table, q, k_new, v_new, cache_k, cache_v, layer, n_tok):
    db, tpad, _ = q.shape
    n_pages = page_table.shape[1]
    page = cache_k.shape[2]
    pages_per_blk = MOBA_BLOCK // page
    assert pages_per_blk == 2, "two cache pages per MoBA block expected"
    n_blocks = n_pages // pages_per_blk
    assert n_blocks <= LANES
    rows = n_tok * ATT_HEADS
    cpage = lambda off: pl.BlockSpec((None, None, page, ATT_W),
                                     lambda bi, n, pt, off=off: (pt[bi, 2 * n + off], layer, 0, 0))
    tok = pl.BlockSpec((None, tpad, ATT_W), lambda bi, n, pt: (bi, 0, 0))
    grid_spec = pltpu.PrefetchScalarGridSpec(
        num_scalar_prefetch=1,
        grid=(db, n_blocks),
        in_specs=[tok, tok, tok, cpage(0), cpage(1), cpage(0), cpage(1)],
        out_specs=tok,
        scratch_shapes=[pltpu.VMEM((LANES, ATT_W), F32),
                        pltpu.VMEM((rows, LANES), F32),
                        pltpu.VMEM((rows, LANES), F32),
                        pltpu.VMEM((n_blocks, rows, ATT_W), F32)],
    )
    return pl.pallas_call(
        functools.partial(_attn_sample_kernel, n_tok=n_tok, n_blocks=n_blocks, tpad=tpad),
        grid_spec=grid_spec,
        out_shape=jax.ShapeDtypeStruct((db, tpad, ATT_W), F32),
        compiler_params=_params(2),
        name="attn_sample",
    )(page_table, q, k_new, v_new, cache_k, cache_k, cache_v, cache_v)


def _gla_kernel(q_ref, f_ref, i_ref, g_ref, lbl_ref, gain_ref, s0_ref, o_ref, so_ref,
                st_ref, lc_ref, kk_ref, *, layer, tb, c, t_valid, n_t):
    t = pl.program_id(1)
    half = MXU_DIM

    @pl.when(t == 0)
    def _():
        st_ref[...] = s0_ref[...]

    lg = lbl_ref[...]
    e = jnp.exp(lg - jnp.max(lg, axis=0, keepdims=True))
    den = jnp.sum(e, axis=0, keepdims=True)
    lb = jnp.zeros((1, HG_W), F32)
    for j in range(1, layer + 1):
        lb = lb + e[j:j + 1] / den

    fr = f_ref[...]
    f = lb + (1.0 - lb) * _sigmoid(fr)
    logf = jnp.log(f)
    kk = (1.0 - lb) * _sigmoid(-fr)
    if t_valid < tb:
        live = lax.broadcasted_iota(jnp.int32, (tb, HG_W), 0) < t_valid
        logf = jnp.where(live, logf, 0.0)
        kk = jnp.where(live, kk, 0.0)
    kk_ref[...] = kk

    grp = min(LANES, tb)
    ri = lax.broadcasted_iota(jnp.int32, (grp, grp), 0)
    ci = lax.broadcasted_iota(jnp.int32, (grp, grp), 1)
    tri = jnp.where((ri // c == ci // c) & (ci <= ri), 1.0, 0.0)
    for r in range(tb // grp):
        lc_ref[r * grp:(r + 1) * grp, :] = jnp.dot(tri, logf[r * grp:(r + 1) * grp], precision=HIGHEST,
                                                   preferred_element_type=F32)

    bi = lax.broadcasted_iota(jnp.int32, (half, half), 0) // HG_DK
    bj = lax.broadcasted_iota(jnp.int32, (half, half), 1) // HG_DK
    bd = jnp.where(bi == bj, 1.0, 0.0)
    bd16 = bd.astype(BF16)
    rows = lax.broadcasted_iota(jnp.int32, (c, HG_W), 0)

    def chunk(cidx, carry):
        r0 = pl.multiple_of(cidx * c, c)
        qc = q_ref[pl.ds(r0, c), :]
        vc = i_ref[pl.ds(r0, c), :]
        kc = kk_ref[pl.ds(r0, c), :]
        lc = lc_ref[pl.ds(r0, c), :]
        last = lc[c - 1:c, :]
        qt = (qc * jnp.exp(lc)).astype(BF16)
        o = jnp.concatenate(
            [lax.dot_general(qt[:, g * half:(g + 1) * half], st_ref[g].astype(BF16), CONTRACT_LAST,
                             preferred_element_type=F32) for g in range(2)], axis=1)
        ws = []
        for s in range(c):
            d = jnp.minimum(lc - lc[s:s + 1, :], 0.0)
            ws.append(jnp.where(rows >= s, qc * kc[s:s + 1, :] * jnp.exp(d), 0.0))
        w = jnp.concatenate(ws, axis=0).astype(BF16)
        att = jnp.concatenate(
            [jnp.dot(w[:, g * half:(g + 1) * half], bd16, preferred_element_type=F32) for g in range(2)], axis=1)
        for s in range(c):
            o = o + att[s * c:(s + 1) * c] * vc[s:s + 1, :]
        o_ref[pl.ds(r0, c), :] = o
        k2 = (kc * jnp.exp(last - lc)).astype(BF16)
        vb = vc.astype(BF16)
        dl = jnp.exp(last)
        for g in range(2):
            upd = lax.dot_general(vb[:, g * half:(g + 1) * half], k2[:, g * half:(g + 1) * half], CONTRACT_FIRST,
                                  preferred_element_type=F32)
            st_ref[g] = st_ref[g] * dl[:, g * half:(g + 1) * half] + bd * upd
        return carry

    lax.fori_loop(0, tb // c, chunk, 0)

    o = o_ref[...]
    o2 = o * o
    ms = jnp.concatenate(
        [jnp.dot(o2[:, g * half:(g + 1) * half], bd, precision=HIGHEST, preferred_element_type=F32)
         for g in range(2)], axis=1) * (1.0 / HG_DK)
    og = g_ref[...]
    o_ref[...] = o * lax.rsqrt(ms + RMS_EPS) * gain_ref[...] * (og * _sigmoid(og))

    @pl.when(t == n_t - 1)
    def _():
        so_ref[...] = st_ref[...]


def _gla_call(h3, lb_logits, gain, s0_bd, layer, tb, c, t_valid):
    b, s, _ = h3.shape
    n_t = s // tb
    n_layers = lb_logits.shape[0]
    col = lambda cidx: pl.BlockSpec((None, tb, COLW), lambda bi, ti, cidx=cidx: (bi, ti, cidx))
    full2 = lambda shape: pl.BlockSpec(shape, lambda bi, ti: (0, 0))
    st_spec = pl.BlockSpec((None, 2, MXU_DIM, MXU_DIM), lambda bi, ti: (bi, 0, 0, 0))
    return pl.pallas_call(
        functools.partial(_gla_kernel, layer=layer, tb=tb, c=c, t_valid=t_valid, n_t=n_t),
        grid=(b, n_t),
        in_specs=[col(C_QH), col(C_FH), col(C_IH), col(C_GH),
                  full2((n_layers, HG_W)), full2((1, HG_W)), st_spec],
        out_specs=[pl.BlockSpec((None, tb, HG_W), lambda bi, ti: (bi, ti, 0)), st_spec],
        out_shape=[jax.ShapeDtypeStruct((b, s, HG_W), F32),
                   jax.ShapeDtypeStruct((b, 2, MXU_DIM, MXU_DIM), F32)],
        scratch_shapes=[pltpu.VMEM((2, MXU_DIM, MXU_DIM), F32),
                        pltpu.VMEM((tb, HG_W), F32),
                        pltpu.VMEM((tb, HG_W), F32)],
        compiler_params=_params(2),
        name="hgrn2",
    )(h3, h3, h3, h3, lb_logits, gain, s0_bd)


def _state_to_bd(s):
    b = s.shape[0]
    st = jnp.swapaxes(s, -1, -2).reshape(b, 2, 4, HG_DK, HG_DK)
    out = jnp.zeros((b, 2, 4, HG_DK, 4, HG_DK), F32)
    for hh in range(4):
        out = out.at[:, :, hh, :, hh, :].set(st[:, :, hh])
    return out.reshape(b, 2, MXU_DIM, MXU_DIM)


def _state_from_bd(st):
    b = st.shape[0]
    st6 = st.reshape(b, 2, 4, HG_DK, 4, HG_DK)
    blocks = jnp.stack([st6[:, :, hh, :, hh, :] for hh in range(4)], axis=2)
    return jnp.swapaxes(blocks.reshape(b, HG_HEADS, HG_DK, HG_DK), -1, -2)


def _lru_kernel(x_ref, g_ref, cw_ref, cb_ref, wa_ref, ba_ref, wx_ref, bx_ref, lam_ref, h0_ref, buf0_ref,
                y_ref, hl_ref, bo_ref, xp_ref, a_ref, u_ref, hc_ref, *, tb, t_valid, pad, n_t):
    t = pl.program_id(1)
    nprev = CONV_W - 1
    base = SUBLANES

    @pl.when(t == 0)
    def _():
        xp_ref[base - nprev:base, :] = buf0_ref[...]
        hc_ref[...] = h0_ref[...]

    x = x_ref[...]
    xp_ref[base:base + tb, :] = x
    cw = cw_ref[...]
    xc = cb_ref[...] + cw[nprev:nprev + 1, :] * x
    for j in range(nprev):
        xc = xc + cw[j:j + 1, :] * xp_ref[base - nprev + j:base - nprev + j + tb, :]
    bo_ref[...] = xp_ref[base + t_valid - nprev:base + t_valid, :]
    xp_ref[base - nprev:base, :] = xp_ref[base + tb - nprev:base + tb, :]

    xb = xc.astype(BF16)
    r = _sigmoid(jnp.dot(xb, wa_ref[...], preferred_element_type=F32) + ba_ref[...])
    ig = _sigmoid(jnp.dot(xb, wx_ref[...], preferred_element_type=F32) + bx_ref[...])
    z = -lam_ref[...]
    softplus = jnp.maximum(z, 0.0) + jnp.log(1.0 + jnp.exp(-jnp.abs(z)))
    log_a = -LRU_C * r * softplus
    a = jnp.exp(log_a)
    u = jnp.sqrt(jnp.maximum(1.0 - jnp.exp(2.0 * log_a), SQRT_EPS)) * ig * xc
    first = lax.broadcasted_iota(jnp.int32, (tb, LRU_W), 0) == 0
    u = u + jnp.where(first, a * hc_ref[...], 0.0)

    a_ref[0:pad, :] = jnp.ones((pad, LRU_W), F32)
    u_ref[0:pad, :] = jnp.zeros((pad, LRU_W), F32)
    a_ref[pad:pad + tb, :] = a
    u_ref[pad:pad + tb, :] = u
    d = 1
    while d < tb:
        a_cur = a_ref[pad:pad + tb, :]
        u_cur = u_ref[pad:pad + tb, :]
        a_sh = a_ref[pad - d:pad - d + tb, :]
        u_sh = u_ref[pad - d:pad - d + tb, :]
        u_ref[pad:pad + tb, :] = a_cur * u_sh + u_cur
        a_ref[pad:pad + tb, :] = a_cur * a_sh
        d *= 2
    h = u_ref[pad:pad + tb, :]
    y_ref[...] = h * _gelu(g_ref[...])
    hc_ref[...] = h[tb - 1:tb, :]
    hl_ref[...] = h[t_valid - 1:t_valid, :]


def _lru_call(h3, w, h0, buf0, tb, t_valid):
    b, s, _ = h3.shape
    n_t = s // tb
    pad = max(SUBLANES, tb // 2)
    cw, cb, wa_bd, ba, wx_bd, bx, lam = w
    col = lambda cidx: pl.BlockSpec((None, tb, COLW), lambda bi, ti, cidx=cidx: (bi, ti, cidx))
    full2 = lambda shape: pl.BlockSpec(shape, lambda bi, ti: (0, 0))
    vec = full2((1, LRU_W))
    per_b = lambda r: pl.BlockSpec((None, r, LRU_W), lambda bi, ti: (bi, 0, 0))
    return pl.pallas_call(
        functools.partial(_lru_kernel, tb=tb, t_valid=t_valid, pad=pad, n_t=n_t),
        grid=(b, n_t),
        in_specs=[col(C_XL), col(C_GL), full2((CONV_W, LRU_W)), vec, full2((LRU_W, LRU_W)), vec,
                  full2((LRU_W, LRU_W)), vec, vec, per_b(1), per_b(CONV_W - 1)],
        out_specs=[pl.BlockSpec((None, tb, LRU_W), lambda bi, ti: (bi, ti, 0)), per_b(1), per_b(CONV_W - 1)],
        out_shape=[jax.ShapeDtypeStruct((b, s, LRU_W), F32),
                   jax.ShapeDtypeStruct((b, 1, LRU_W), F32),
                   jax.ShapeDtypeStruct((b, CONV_W - 1, LRU_W), F32)],
        scratch_shapes=[pltpu.VMEM((SUBLANES + tb, LRU_W), F32),
                        pltpu.VMEM((pad + tb, LRU_W), F32),
                        pltpu.VMEM((pad + tb, LRU_W), F32),
                        pltpu.VMEM((1, LRU_W), F32)],
        compiler_params=_params(2),
        name="rglru",
    )(h3, h3, cw, cb, wa_bd, ba, wx_bd, bx, lam, h0, buf0)


def _block_diag(w):
    out = jnp.zeros((LRU_BLOCKS, LRU_BD, LRU_BLOCKS, LRU_BD), w.dtype)
    for n in range(LRU_BLOCKS):
        out = out.at[n, :, n, :].set(w[n])
    return out.reshape(LRU_W, LRU_W)


def _merge_kernel(x_ref, oa_ref, oh_ref, ol_ref, g0a, g0b, g1a, g1b, g2a, g2b,
                  wa_ref, wh_ref, wl_ref, wo_ref, lg_ref, lb_ref, y_ref, *, alpha):
    def branch(o_ref, w_ref, ga, gb):
        y = jnp.dot(o_ref[...].astype(BF16), w_ref[...], preferred_element_type=F32)
        gate = jnp.concatenate([ga[...], gb[...]], axis=1)
        return _sigmoid(gate) * y

    merged = branch(oa_ref, wa_ref, g0a, g0b) + branch(oh_ref, wh_ref, g1a, g1b) + branch(ol_ref, wl_ref, g2a, g2b)
    mix = jnp.dot(merged.astype(BF16), wo_ref[...], preferred_element_type=F32)
    y_ref[...] = _layer_norm(alpha * x_ref[...] + mix, lg_ref[...], lb_ref[...])


def _merge_call(x2, h2, o_att, o_hg, o_lru, w, tm, alpha):
    m = x2.shape[0]
    w_att, w_hg, w_lru, w_o, ln_g, ln_b = w
    row = lambda width: pl.BlockSpec((tm, width), lambda i: (i, 0))
    col = lambda cidx: pl.BlockSpec((tm, COLW), lambda i, cidx=cidx: (i, cidx))
    full = lambda shape: pl.BlockSpec(shape, lambda i: (0, 0))
    return pl.pallas_call(
        functools.partial(_merge_kernel, alpha=alpha),
        grid=(m // tm,),
        in_specs=[row(D_MODEL), row(ATT_W), row(HG_W), row(LRU_W)]
                 + [col(C_GM + j) for j in range(6)]
                 + [full((ATT_W, D_MODEL)), full((HG_W, D_MODEL)), full((LRU_W, D_MODEL)),
                    full((D_MODEL, D_MODEL)), full((1, D_MODEL)), full((1, D_MODEL))],
        out_specs=row(D_MODEL),
        out_shape=jax.ShapeDtypeStruct((m, D_MODEL), F32),
        compiler_params=_params(1),
        name="merge_ln",
    )(x2, o_att, o_hg, o_lru, h2, h2, h2, h2, h2, h2, w_att, w_hg, w_lru, w_o, ln_g, ln_b)


def _ffn_kernel(x_ref, wup_ref, cw_ref, cb_ref, wdn_ref, lg_ref, lb_ref, p2_ref, p1_ref,
                y_ref, tail_ref, carry_ref, *, tm, ts, fc, alpha, per_row_prev, n_t):
    i = pl.program_id(0)
    x = x_ref[...]
    xb = x.astype(BF16)
    tpos = lax.broadcasted_iota(jnp.int32, (tm, fc), 0) % ts
    cw = cw_ref[...]
    cb = cb_ref[...]

    if not per_row_prev:
        @pl.when(i % n_t == 0)
        def _():
            carry_ref[...] = p2_ref[...]

    acc = jnp.zeros((tm, D_MODEL), F32)
    for cidx in range(D_FF // fc):
        lo, hi = cidx * fc, (cidx + 1) * fc
        u = jnp.dot(xb, wup_ref[:, lo:hi], preferred_element_type=F32)
        val = jnp.dot(xb, wup_ref[:, D_FF + lo:D_FF + hi], preferred_element_type=F32)
        if per_row_prev:
            prev2 = p2_ref[:, lo:hi]
            prev1 = p1_ref[:, lo:hi]
        else:
            cm2 = carry_ref[0:1, lo:hi]
            cm1 = carry_ref[1:2, lo:hi]
            prev2 = jnp.where(tpos == 0, cm2, cm1)
            prev1 = jnp.broadcast_to(cm1, (tm, fc))
        u1 = jnp.where(tpos >= 1, pltpu.roll(u, 1, 0), prev1)
        u2 = jnp.where(tpos >= 2, pltpu.roll(u, 2, 0), prev2)
        uc = cb[:, lo:hi] + cw[0:1, lo:hi] * u2 + cw[1:2, lo:hi] * u1 + cw[2:3, lo:hi] * u
        if per_row_prev:
            tail_ref[:, lo:hi] = u
        else:
            carry_ref[:, lo:hi] = u[tm - 2:tm, :]
            tail_ref[:, lo:hi] = u[tm - SUBLANES:tm, :]
        gated = (_gelu(uc) * val).astype(BF16)
        acc = acc + jnp.dot(gated, wdn_ref[lo:hi, :], preferred_element_type=F32)
    y_ref[...] = _layer_norm(alpha * x + acc, lg_ref[...], lb_ref[...])


def _ffn_call(x2, w, prev2, prev1, tm, ts, n_seq, alpha, per_row_prev):
    m = x2.shape[0]
    w_up, cw, cb, w_dn, ln_g, ln_b = w
    n_t = (m // n_seq) // tm if not per_row_prev else 1
    fc = 1024
    row = pl.BlockSpec((tm, D_MODEL), lambda i: (i, 0))
    full = lambda shape: pl.BlockSpec(shape, lambda i: (0, 0))
    resident = lambda shape: pl.BlockSpec(shape, lambda i: (0, 0), pipeline_mode=pl.Buffered(1))
    if per_row_prev:
        p2_spec = pl.BlockSpec((tm, D_FF), lambda i: (i, 0))
        p1_spec = pl.BlockSpec((tm, D_FF), lambda i: (i, 0))
        tail_spec = pl.BlockSpec((tm, D_FF), lambda i: (i, 0))
        tail_shape = jax.ShapeDtypeStruct((m, D_FF), F32)
    else:
        p2_spec = pl.BlockSpec((None, FFN_CONV_W - 1, D_FF), lambda i: (i // n_t, 0, 0))
        p1_spec = pl.BlockSpec((None, FFN_CONV_W - 1, D_FF), lambda i: (i // n_t, 0, 0))
        tail_spec = pl.BlockSpec((None, SUBLANES, D_FF), lambda i: (i // n_t, 0, 0))
        tail_shape = jax.ShapeDtypeStruct((n_seq, SUBLANES, D_FF), F32)
    return pl.pallas_call(
        functools.partial(_ffn_kernel, tm=tm, ts=ts, fc=fc, alpha=alpha, per_row_prev=per_row_prev, n_t=n_t),
        grid=(m // tm,),
        in_specs=[row, resident((D_MODEL, 2 * D_FF)), full((FFN_CONV_W, D_FF)), full((1, D_FF)),
                  resident((D_FF, D_MODEL)), full((1, D_MODEL)), full((1, D_MODEL)), p2_spec, p1_spec],
        out_specs=[row, tail_spec],
        out_shape=[jax.ShapeDtypeStruct((m, D_MODEL), F32), tail_shape],
        scratch_shapes=[pltpu.VMEM((FFN_CONV_W - 1, D_FF), F32)],
        compiler_params=_params(1),
        name="ffn_ln",
    )(x2, w_up, cw, cb, w_dn, ln_g, ln_b, prev2, prev1)


def _trunk_layer(x3, tabs, attn_fn, hg_s0, lru_h0, lru_buf0, ffn_buf0, wl, layer, cfg):
    b, s, _ = x3.shape
    m = b * s
    x2 = x3.reshape(m, D_MODEL)
    h2 = _matmul(x2, wl["w_in"], cfg["bm"], cfg["bn"])
    h3 = h2.reshape(b, s, D_IN)
    rope_out = _rope_call(h3, tabs, cfg["rope_tm"], cfg["prompt"])
    q_rot, k_rot, v_f32, k_bf, v_bf = rope_out[:5]
    ksum = rope_out[5] if cfg["prompt"] else None
    o_att = attn_fn(q_rot, k_rot, v_f32, k_bf, v_bf, ksum)
    o_hg, hg_st = _gla_call(h3, wl["hg_lb_logits"], wl["hg_gain"], hg_s0, layer,
                            cfg["seq_tb"], cfg["gla_c"], cfg["t_valid"])
    o_lru, lru_h, lru_buf = _lru_call(h3, wl["lru"], lru_h0, lru_buf0, cfg["seq_tb"], cfg["t_valid"])
    x1 = _merge_call(x2, h2, o_att.reshape(m, ATT_W), o_hg.reshape(m, HG_W), o_lru.reshape(m, LRU_W),
                     wl["merge"], cfg["tok_tm"], cfg["alpha"])
    if cfg["prompt"]:
        x_out, tail = _ffn_call(x1, wl["ffn"], ffn_buf0, ffn_buf0, cfg["tok_tm"], cfg["tok_tm"], b,
                                cfg["alpha"], False)
        ffn_buf = tail[:, SUBLANES - (FFN_CONV_W - 1):, :]
    else:
        tv = cfg["t_valid"]
        zeros = jnp.zeros((b, s, D_FF), F32)
        prev2 = zeros.at[:, 0].set(ffn_buf0[:, 0]).at[:, 1].set(ffn_buf0[:, 1]).reshape(m, D_FF)
        prev1 = zeros.at[:, 0].set(ffn_buf0[:, 1]).reshape(m, D_FF)
        x_out, u_all = _ffn_call(x1, wl["ffn"], prev2, prev1, m, s, b, cfg["alpha"], True)
        ffn_buf = u_all.reshape(b, s, D_FF)[:, tv - (FFN_CONV_W - 1):tv]
    return (x_out.reshape(b, s, D_MODEL), k_rot, v_f32, hg_st, lru_h.reshape(b, LRU_W), lru_buf, ffn_buf)


def kernel(x_prompt, x_sample, cache_k, cache_v, page_table, state_hgrn, state_lru_h, state_lru_conv, state_ffn_conv, w_in, hg_lb_logits, hg_gain, lru_conv_w, lru_conv_b, lru_wa, lru_ba, lru_wx, lru_bx, lru_lambda, w_br_att, w_br_hg, w_br_lru, w_o, ln1_g, ln1_b, ffn_w_up, ffn_conv_w, ffn_conv_b, ffn_w_down, ln2_g, ln2_b):
    bsz, seq, _ = x_prompt.shape
    dbs, n_tok, _ = x_sample.shape
    depth = w_in.shape[0]
    n_pool, _, page, _, _ = cache_k.shape
    past = page_table.shape[1] * page
    assert seq % MOBA_BLOCK == 0 and past % MOBA_BLOCK == 0
    assert CONV_W - 1 <= n_tok <= SUBLANES
    alpha = (2.0 * depth) ** 0.25
    tpad = SUBLANES

    tabs_p = _rope_tables(jnp.arange(seq))
    tabs_s = _rope_tables(past + jnp.arange(tpad))
    ck = cache_k.reshape(n_pool, depth, page, ATT_W)
    cv = cache_v.reshape(n_pool, depth, page, ATT_W)

    seq_tb = min(512, seq)
    cfg_p = dict(prompt=True, bm=min(1024, bsz * seq), bn=1536, rope_tm=min(512, seq), seq_tb=seq_tb,
                 gla_c=16, t_valid=seq_tb, tok_tm=min(512, seq), alpha=alpha)
    cfg_s = dict(prompt=False, bm=dbs * tpad, bn=1536, rope_tm=tpad, seq_tb=tpad,
                 gla_c=tpad, t_valid=n_tok, tok_tm=dbs * tpad, alpha=alpha)

    xp = x_prompt
    xs = jnp.pad(x_sample, ((0, 0), (0, tpad - n_tok), (0, 0)))
    hg0_p = jnp.zeros((bsz, 2, MXU_DIM, MXU_DIM), F32)
    lh0_p = jnp.zeros((bsz, 1, LRU_W), F32)
    lb0_p = jnp.zeros((bsz, CONV_W - 1, LRU_W), F32)
    fb0_p = jnp.zeros((bsz, FFN_CONV_W - 1, D_FF), F32)

    outs_p = [[] for _ in range(6)]
    outs_s = [[] for _ in range(6)]
    row = lambda a: a.reshape(1, -1)
    for l in range(depth):
        wl = dict(
            w_in=w_in[l].astype(BF16),
            hg_lb_logits=hg_lb_logits,
            hg_gain=row(hg_gain[l]),
            lru=(lru_conv_w[l], row(lru_conv_b[l]), _block_diag(lru_wa[l]).astype(BF16), row(lru_ba[l]),
                 _block_diag(lru_wx[l]).astype(BF16), row(lru_bx[l]), row(lru_lambda[l])),
            merge=(w_br_att[l].astype(BF16), w_br_hg[l].astype(BF16), w_br_lru[l].astype(BF16),
                   w_o[l].astype(BF16), row(ln1_g[l]), row(ln1_b[l])),
            ffn=(ffn_w_up[l].astype(BF16), ffn_conv_w[l], row(ffn_conv_b[l]), ffn_w_down[l].astype(BF16),
                 row(ln2_g[l]), row(ln2_b[l])),
        )

        def attn_p(q_rot, k_rot, v_f32, k_bf, v_bf, ksum):
            n_blk = ksum.shape[1]
            ks = jnp.pad(ksum.reshape(bsz, n_blk, ATT_W), ((0, 0), (0, LANES - n_blk % LANES if n_blk % LANES else 0), (0, 0)))
            return _attn_prompt_call(q_rot, k_bf, v_bf, ks)

        def attn_s(q_rot, k_rot, v_f32, k_bf, v_bf, ksum, l=l):
            return _attn_sample_call(page_table, q_rot, k_rot, v_f32, ck, cv, l, n_tok)

        xp, *new_p = _trunk_layer(xp, tabs_p, attn_p, hg0_p, lh0_p, lb0_p, fb0_p, wl, l, cfg_p)
        xs, *new_s = _trunk_layer(xs, tabs_s, attn_s, _state_to_bd(state_hgrn[:, l]),
                                  state_lru_h[:, l].reshape(dbs, 1, LRU_W), state_lru_conv[:, l],
                                  state_ffn_conv[:, l], wl, l, cfg_s)
        for lst, a in zip(outs_p, new_p):
            lst.append(a)
        for lst, a in zip(outs_s, new_s):
            lst.append(a)

    def heads(a, t):
        return a.reshape(a.shape[0], a.shape[1], t, ATT_HEADS, HEAD_DIM)

    k_p = heads(jnp.stack(outs_p[0], axis=1), seq)
    v_p = heads(jnp.stack(outs_p[1], axis=1), seq)
    k_s = heads(jnp.stack([a[:, :n_tok] for a in outs_s[0]], axis=1), n_tok)
    v_s = heads(jnp.stack([a[:, :n_tok] for a in outs_s[1]], axis=1), n_tok)
    hg_p = jnp.stack([_state_from_bd(a) for a in outs_p[2]], axis=1)
    hg_s = jnp.stack([_state_from_bd(a) for a in outs_s[2]], axis=1)
    lh_p = jnp.stack(outs_p[3], axis=1)
    lh_s = jnp.stack(outs_s[3], axis=1)
    lc_p = jnp.stack(outs_p[4], axis=1)
    lc_s = jnp.stack(outs_s[4], axis=1)
    fc_p = jnp.stack(outs_p[5], axis=1)
    fc_s = jnp.stack(outs_s[5], axis=1)
    return (xp, xs[:, :n_tok], k_p, v_p, k_s, v_s, hg_p, hg_s, lh_p, lh_s, lc_p, lc_s, fc_p, fc_s)
```

```python
import functools
import math

import jax
import jax.numpy as jnp
from jax import lax
from jax.experimental import pallas as pl
from jax.experimental.pallas import tpu as pltpu

F32 = jnp.float32
BF16 = jnp.bfloat16

D_MODEL = 1024
ATT_HEADS = 8
HEAD_DIM = 64
ATT_W = ATT_HEADS * HEAD_DIM
MOBA_BLOCK = 256
MOBA_TOPK = 3
ROPE_THETA = 500000.0
ROPE_DIM = HEAD_DIM // 4
HG_HEADS = 8
HG_DK = 64
HG_W = HG_HEADS * HG_DK
LRU_W = 512
LRU_BLOCKS = 8
LRU_BD = LRU_W // LRU_BLOCKS
LRU_C = 8.0
CONV_W = 4
D_FF = 3 * D_MODEL
FFN_CONV_W = 3
N_BRANCH = 3
LN_EPS = 1e-5
RMS_EPS = 1e-6
SQRT_EPS = 1e-12
NEG_INF = -1e30
D_IN = 3 * ATT_W + 4 * HG_W + 2 * LRU_W + N_BRANCH * D_MODEL
LOG2E = math.log2(math.e)
ATTN_GROUP = 2

COLW = 512
C_QA, C_KA, C_VA, C_QH, C_FH, C_IH, C_GH, C_XL, C_GL, C_GM = 0, 1, 2, 3, 4, 5, 6, 7, 8, 9

LANES = 128
SUBLANES = 8
MXU_DIM = 256
VMEM_LIMIT = 56 * 1024 * 1024

HIGHEST = lax.Precision.HIGHEST
CONTRACT_LAST = (((1,), (1,)), ((), ()))
CONTRACT_FIRST = (((0,), (0,)), ((), ()))


def _params(n_grid):
    return pltpu.CompilerParams(dimension_semantics=("arbitrary",) * n_grid,
                                vmem_limit_bytes=VMEM_LIMIT)


def _sigmoid(x):
    return 1.0 / (1.0 + jnp.exp(-x))


def _gelu(x):
    c = math.sqrt(2.0 / math.pi)
    return x * (0.5 * (1.0 + jnp.tanh(c * (x + 0.044715 * (x * x * x)))))


def _layer_norm(y, g, b):
    mu = jnp.mean(y, axis=-1, keepdims=True)
    d = y - mu
    var = jnp.mean(d * d, axis=-1, keepdims=True)
    return d * lax.rsqrt(var + LN_EPS) * g + b


def _split_dot(x, w01, parts):
    w = w01.astype(BF16)
    acc = None
    rem = x
    for _ in range(parts):
        piece = rem.astype(BF16)
        rem = rem - piece.astype(F32)
        y = jnp.dot(piece, w, preferred_element_type=F32)
        acc = y if acc is None else acc + y
    return acc


def _top_k_mask(gate, valid, axis):
    idx_all = lax.broadcasted_iota(jnp.int32, gate.shape, axis).astype(F32)
    g = jnp.where(valid, gate, -jnp.inf)
    sel = jnp.zeros(gate.shape, F32)
    for _ in range(MOBA_TOPK):
        mx = jnp.max(g, axis=axis, keepdims=True)
        idx = jnp.min(jnp.where(g == mx, idx_all, float(gate.shape[axis])), axis=axis, keepdims=True)
        pick = idx_all == idx
        sel = jnp.where(pick, 1.0, sel)
        g = jnp.where(pick, -jnp.inf, g)
    return jnp.where(valid, sel, 0.0)


def _mm_kernel(x_ref, w_ref, o_ref):
    o_ref[...] = jnp.dot(x_ref[...].astype(BF16), w_ref[...], preferred_element_type=F32)


def _matmul(x, w, bm, bn):
    m, k = x.shape
    n = w.shape[1]
    assert m % bm == 0 and n % bn == 0
    return pl.pallas_call(
        _mm_kernel,
        grid=(m // bm, n // bn),
        in_specs=[pl.BlockSpec((bm, k), lambda i, j: (i, 0)),
                  pl.BlockSpec((k, bn), lambda i, j: (0, j))],
        out_specs=pl.BlockSpec((bm, bn), lambda i, j: (i, j)),
        out_shape=jax.ShapeDtypeStruct((m, n), F32),
        compiler_params=_params(2),
        name="in_proj",
    )(x, w)


def _rope_kernel(q_ref, k_ref, v_ref, c_ref, sa_ref, sb_ref, qo_ref, ko_ref, vo_ref, *extra, n_blk):
    c = c_ref[...]
    sa = sa_ref[...]
    sb = sb_ref[...]

    def rot(x):
        outs = []
        for j in range(ATT_W // LANES):
            xs = x[:, LANES * j:LANES * (j + 1)]
            outs.append(xs * c + pltpu.roll(xs, ROPE_DIM // 2, 1) * sa
                        + pltpu.roll(xs, LANES - ROPE_DIM // 2, 1) * sb)
        return jnp.concatenate(outs, axis=1)

    q = rot(q_ref[...])
    k = rot(k_ref[...])
    v = v_ref[...]
    qo_ref[...] = q
    ko_ref[...] = k
    vo_ref[...] = v
    if n_blk:
        kb_ref, vt_ref, ks_ref = extra
        kb_ref[...] = k.astype(BF16)
        for r in range(n_blk):
            rows = slice(MOBA_BLOCK * r, MOBA_BLOCK * (r + 1))
            ks_ref[r] = jnp.sum(k[rows], axis=0, keepdims=True)
            vt_ref[r] = v[rows].T.astype(BF16)


def _rope_call(h3, tabs, tm, prompt):
    b, s, _ = h3.shape
    nt = s // tm
    n_blk = tm // MOBA_BLOCK if prompt else 0
    col = lambda cidx: pl.BlockSpec((None, tm, COLW), lambda bi, ti, cidx=cidx: (bi, ti, cidx))
    tab = pl.BlockSpec((tm, LANES), lambda bi, ti: (ti, 0))
    row = pl.BlockSpec((None, tm, ATT_W), lambda bi, ti: (bi, ti, 0))
    out_specs = [row, row, row]
    out_shape = [jax.ShapeDtypeStruct((b, s, ATT_W), F32)] * 3
    if prompt:
        out_specs += [row,
                      pl.BlockSpec((None, n_blk, ATT_W, MOBA_BLOCK), lambda bi, ti: (bi, ti, 0, 0)),
                      pl.BlockSpec((None, n_blk, 1, ATT_W), lambda bi, ti: (bi, ti, 0, 0))]
        out_shape += [jax.ShapeDtypeStruct((b, s, ATT_W), BF16),
                      jax.ShapeDtypeStruct((b, s // MOBA_BLOCK, ATT_W, MOBA_BLOCK), BF16),
                      jax.ShapeDtypeStruct((b, s // MOBA_BLOCK, 1, ATT_W), F32)]
    return pl.pallas_call(
        functools.partial(_rope_kernel, n_blk=n_blk),
        grid=(b, nt),
        in_specs=[col(C_QA), col(C_KA), col(C_VA), tab, tab, tab],
        out_specs=out_specs,
        out_shape=out_shape,
        compiler_params=_params(2),
        name="rope",
    )(h3, h3, h3, *tabs)


def _rope_tables(pos):
    half = ROPE_DIM // 2
    inv = ROPE_THETA ** (-jnp.arange(half, dtype=F32) / half)
    ang = pos.astype(F32)[:, None] * inv[None, :]
    cos, sin = jnp.cos(ang), jnp.sin(ang)
    lh = jnp.arange(LANES) % HEAD_DIM
    fi = lh % half
    c = jnp.where(lh[None, :] < ROPE_DIM, cos[:, fi], 1.0)
    sa = jnp.where((lh[None, :] >= half) & (lh[None, :] < ROPE_DIM), sin[:, fi], 0.0)
    sb = jnp.where(lh[None, :] < half, -sin[:, fi], 0.0)
    return c.astype(F32), sa.astype(F32), sb.astype(F32)


def _attn_prompt_kernel(q_ref, k_ref, vt_ref, ks_ref, o_ref, bias_ref, sa_ref, sb_ref, *, tq, nbp, nb, n_blk):
    i = pl.program_id(2)
    qscale = HEAD_DIM ** -0.5 * LOG2E
    q = q_ref[...]
    kmean = ks_ref[...] * (1.0 / MOBA_BLOCK)
    lane = lax.broadcasted_iota(jnp.int32, (tq, LANES), 1)
    valid = lax.broadcasted_iota(jnp.int32, (nbp, tq), 0) < i
    causal = (lax.broadcasted_iota(jnp.int32, (tq, tq), 0) <= lax.broadcasted_iota(jnp.int32, (tq, tq), 1))
    own0 = pl.multiple_of(i * tq, tq)
    k_own = k_ref[pl.ds(own0, tq), :]
    vt_own = vt_ref[i]

    qs = []
    state = []
    for hh in range(2):
        in_head = (lane >= hh * HEAD_DIM) & (lane < (hh + 1) * HEAD_DIM)
        qh = jnp.where(in_head, q, 0.0)
        gate_t = lax.dot_general(kmean, qh, CONTRACT_LAST, precision=HIGHEST, preferred_element_type=F32)
        sel = _top_k_mask(gate_t, valid, 0)
        bias_ref[hh, 0:nbp, :] = jnp.where(sel > 0.0, 0.0, NEG_INF)
        qsh = (qh * qscale).astype(BF16)
        qs.append(qsh)
        s = lax.dot_general(k_own, qsh, CONTRACT_LAST, preferred_element_type=F32)
        s = jnp.where(causal, s, NEG_INF)
        m = jnp.max(s, axis=0, keepdims=True)
        p = jnp.exp2(s - m)
        l = jnp.sum(p, axis=0, keepdims=True)
        acc = jnp.dot(vt_own[hh * HEAD_DIM:(hh + 1) * HEAD_DIM, :], p.astype(BF16), preferred_element_type=F32)
        state += [m, l, acc]

    neg_rows = jnp.full((SUBLANES, tq), NEG_INF, F32)
    bias_ref[0, nbp:nbp + SUBLANES, :] = neg_rows
    bias_ref[1, nbp:nbp + SUBLANES, :] = neg_rows

    def form_scores(jg, s_ref):
        gmax = [None, None]
        for r in range(nb):
            j = jg * nb + r
            jk = jnp.minimum(j, n_blk - 1)
            jb = jnp.minimum(j, nbp)
            kj = k_ref[pl.ds(pl.multiple_of(jk * tq, tq), tq), :]
            for hh in range(2):
                s = (lax.dot_general(kj, qs[hh], CONTRACT_LAST, preferred_element_type=F32)
                     + bias_ref[hh, pl.ds(jb, 1), :])
                s_ref[r * 2 + hh] = s
                mx = jnp.max(s, axis=0, keepdims=True)
                gmax[hh] = mx if gmax[hh] is None else jnp.maximum(gmax[hh], mx)
        return gmax

    def softmax_pv(jg, s_ref, gmax, st):
        out = []
        for hh in range(2):
            m, l, acc = st[3 * hh:3 * hh + 3]
            m_new = jnp.maximum(m, gmax[hh])
            alpha = jnp.exp2(m - m_new)
            l = alpha * l
            acc = alpha * acc
            for r in range(nb):
                vtj = vt_ref[jnp.minimum(jg * nb + r, n_blk - 1)]
                p = jnp.exp2(s_ref[r * 2 + hh] - m_new)
                l = l + jnp.sum(p, axis=0, keepdims=True)
                acc = acc + jnp.dot(vtj[hh * HEAD_DIM:(hh + 1) * HEAD_DIM, :], p.astype(BF16),
                                    preferred_element_type=F32)
            out += [m_new, l, acc]
        return out

    def body(it, carry):
        st, gmax_a = list(carry[:6]), list(carry[6:])
        gmax_b = form_scores(2 * it + 1, sb_ref)
        st = softmax_pv(2 * it, sa_ref, gmax_a, st)
        gmax_a = form_scores(2 * it + 2, sa_ref)
        st = softmax_pv(2 * it + 1, sb_ref, gmax_b, st)
        return tuple(st) + tuple(gmax_a)

    n_groups = (i + nb - 1) // nb
    res = lax.fori_loop(0, (n_groups + 1) // 2, body, tuple(state) + tuple(form_scores(0, sa_ref)))
    st = res[:6]
    o_t = jnp.concatenate([st[2] / st[1], st[5] / st[4]], axis=0)
    o_ref[...] = o_t.T


def _attn_prompt_call(q, kb, vt, ksum):
    b, s, _ = q.shape
    tq = MOBA_BLOCK
    n_blk = s // tq
    nbp = ksum.shape[1]
    nb = math.gcd(n_blk, ATTN_GROUP)
    n_pairs = ATT_W // LANES
    return pl.pallas_call(
        functools.partial(_attn_prompt_kernel, tq=tq, nbp=nbp, nb=nb, n_blk=n_blk),
        grid=(b, n_pairs, n_blk),
        in_specs=[pl.BlockSpec((None, tq, LANES), lambda bi, hp, i: (bi, i, hp)),
                  pl.BlockSpec((None, s, LANES), lambda bi, hp, i: (bi, 0, hp)),
                  pl.BlockSpec((None, n_blk, LANES, tq), lambda bi, hp, i: (bi, 0, hp, 0)),
                  pl.BlockSpec((None, nbp, LANES), lambda bi, hp, i: (bi, 0, hp))],
        out_specs=pl.BlockSpec((None, tq, LANES), lambda bi, hp, i: (bi, i, hp)),
        out_shape=jax.ShapeDtypeStruct((b, s, ATT_W), F32),
        scratch_shapes=[pltpu.VMEM((2, nbp + SUBLANES, tq), F32),
                        pltpu.VMEM((2 * nb, tq, tq), F32),
                        pltpu.VMEM((2 * nb, tq, tq), F32)],
        compiler_params=_params(3),
        name="attn_prompt",
    )(q, kb, vt, ksum)


def _attn_sample_kernel(pt_ref, q_ref, kn_ref, vn_ref, ka_ref, kb_ref, va_ref, vb_ref, o_ref,
                        ksum_ref, m_ref, l_ref, oall_ref, *, n_tok, n_blocks, tpad, page):
    del pt_ref
    n = pl.program_id(1)
    rows = n_tok * ATT_HEADS
    pk = page * ATT_HEADS
    scale = HEAD_DIM ** -0.5
    qm = q_ref[0:n_tok].reshape(rows, HEAD_DIM)
    qs = (qm * scale).astype(BF16)
    lane = lax.broadcasted_iota(jnp.int32, (rows, LANES), 1)

    @pl.when(n == 0)
    def _():
        m_ref[...] = jnp.full(m_ref.shape, NEG_INF, F32)
        l_ref[...] = jnp.zeros(l_ref.shape, F32)

    ka = ka_ref[...]
    kb = kb_ref[...]
    ksum_ref[n] = jnp.sum(ka, axis=0) + jnp.sum(kb, axis=0)
    k2 = jnp.concatenate([ka.reshape(pk, HEAD_DIM), kb.reshape(pk, HEAD_DIM)], axis=0).astype(BF16)
    v2 = jnp.concatenate([va_ref[...].reshape(pk, HEAD_DIM), vb_ref[...].reshape(pk, HEAD_DIM)],
                         axis=0).astype(BF16)
    s = lax.dot_general(qs, k2, CONTRACT_LAST, preferred_element_type=F32)
    same_head = (lax.broadcasted_iota(jnp.int32, (rows, 2 * pk), 0) % ATT_HEADS
                 == lax.broadcasted_iota(jnp.int32, (rows, 2 * pk), 1) % ATT_HEADS)
    s = jnp.where(same_head, s, NEG_INF)
    m_n = jnp.max(s, axis=1, keepdims=True)
    p = jnp.exp(s - m_n)
    l_n = jnp.sum(p, axis=1, keepdims=True)
    oall_ref[n] = jnp.dot(p.astype(BF16), v2, preferred_element_type=F32)
    m_ref[...] = jnp.where(lane == n, m_n, m_ref[...])
    l_ref[...] = jnp.where(lane == n, l_n, l_ref[...])

    @pl.when(n == n_blocks - 1)
    def _():
        nk = n_blocks * ATT_HEADS
        km = ksum_ref[...].reshape(nk, HEAD_DIM) * (1.0 / MOBA_BLOCK)
        g = lax.dot_general(qm, km, CONTRACT_LAST, precision=HIGHEST, preferred_element_type=F32)
        gh = (lax.broadcasted_iota(jnp.int32, (rows, nk), 0) % ATT_HEADS
              == lax.broadcasted_iota(jnp.int32, (rows, nk), 1) % ATT_HEADS)
        pick = jnp.where(lax.broadcasted_iota(jnp.int32, (nk, LANES), 0) // ATT_HEADS
                         == lax.broadcasted_iota(jnp.int32, (nk, LANES), 1), 1.0, 0.0)
        gate = _split_dot(jnp.where(gh, g, 0.0), pick, 3)
        valid = lane < n_blocks
        sel = _top_k_mask(gate, valid, 1) > 0.0
        m_all = m_ref[...]
        l_all = l_ref[...]
        nown = tpad * ATT_HEADS
        kn = kn_ref[...].reshape(nown, HEAD_DIM).astype(BF16)
        vn = vn_ref[...].reshape(nown, HEAD_DIM).astype(BF16)
        s_own = lax.dot_general(qs, kn, CONTRACT_LAST, preferred_element_type=F32)
        r_i = lax.broadcasted_iota(jnp.int32, (rows, nown), 0)
        c_i = lax.broadcasted_iota(jnp.int32, (rows, nown), 1)
        own_ok = ((r_i % ATT_HEADS == c_i % ATT_HEADS) & (c_i // ATT_HEADS <= r_i // ATT_HEADS)
                  & (c_i // ATT_HEADS < n_tok))
        s_own = jnp.where(own_ok, s_own, NEG_INF)
        m_own = jnp.max(s_own, axis=1, keepdims=True)
        m_fin = jnp.maximum(m_own, jnp.max(jnp.where(sel, m_all, NEG_INF), axis=1, keepdims=True))
        p_own = jnp.exp(s_own - m_fin)
        w = jnp.where(sel, jnp.exp(m_all - m_fin), 0.0)
        l_fin = jnp.sum(p_own, axis=1, keepdims=True) + jnp.sum(w * l_all, axis=1, keepdims=True)
        o = jnp.dot(p_own.astype(BF16), vn, preferred_element_type=F32)
        for nb in range(n_blocks):
            o = o + w[:, nb:nb + 1] * oall_ref[nb]
        o_ref[...] = o / l_fin


def _attn_sample_call(page_table, q4, kn4, vn4, cache_k, cache_v, layer, n_tok):
    db, tpad = q4.shape[:2]
    n_pages = page_table.shape[1]
    page = cache_k.shape[2]
    pages_per_blk = MOBA_BLOCK // page
    assert pages_per_blk == 2, "two cache pages per MoBA block expected"
    n_blocks = n_pages // pages_per_blk
    assert n_blocks <= LANES
    rows = n_tok * ATT_HEADS
    cpage = lambda off: pl.BlockSpec((None, None, page, ATT_HEADS, HEAD_DIM),
                                     lambda bi, n, pt, off=off: (pt[bi, 2 * n + off], layer, 0, 0, 0))
    tok = pl.BlockSpec((None, tpad, ATT_HEADS, HEAD_DIM), lambda bi, n, pt: (bi, 0, 0, 0))
    grid_spec = pltpu.PrefetchScalarGridSpec(
        num_scalar_prefetch=1,
        grid=(db, n_blocks),
        in_specs=[tok, tok, tok, cpage(0), cpage(1), cpage(0), cpage(1)],
        out_specs=pl.BlockSpec((None, rows, HEAD_DIM), lambda bi, n, pt: (bi, 0, 0)),
        scratch_shapes=[pltpu.VMEM((n_blocks, ATT_HEADS, HEAD_DIM), F32),
                        pltpu.VMEM((rows, LANES), F32),
                        pltpu.VMEM((rows, LANES), F32),
                        pltpu.VMEM((n_blocks, rows, HEAD_DIM), F32)],
    )
    return pl.pallas_call(
        functools.partial(_attn_sample_kernel, n_tok=n_tok, n_blocks=n_blocks, tpad=tpad, page=page),
        grid_spec=grid_spec,
        out_shape=jax.ShapeDtypeStruct((db, rows, HEAD_DIM), F32),
        compiler_params=_params(2),
        name="attn_sample",
    )(page_table, q4, kn4, vn4, cache_k, cache_k, cache_v, cache_v)


def _gla_kernel(q_ref, f_ref, i_ref, g_ref, lbl_ref, gain_ref, s0_ref, o_ref, so_ref,
                st_ref, lc_ref, kk_ref, *, layer, tb, c, t_valid, n_t):
    t = pl.program_id(1)
    half = MXU_DIM

    @pl.when(t == 0)
    def _():
        st_ref[...] = s0_ref[...]

    lg = lbl_ref[...]
    e = jnp.exp(lg - jnp.max(lg, axis=0, keepdims=True))
    den = jnp.sum(e, axis=0, keepdims=True)
    lb = jnp.zeros((1, HG_W), F32)
    for j in range(1, layer + 1):
        lb = lb + e[j:j + 1] / den

    fr = f_ref[...]
    f = lb + (1.0 - lb) * _sigmoid(fr)
    logf2 = jnp.log2(f)
    kk = (1.0 - lb) * _sigmoid(-fr)
    if t_valid < tb:
        live = lax.broadcasted_iota(jnp.int32, (tb, HG_W), 0) < t_valid
        logf2 = jnp.where(live, logf2, 0.0)
        kk = jnp.where(live, kk, 0.0)
    kk_ref[...] = kk

    grp = min(LANES, tb)
    ri = lax.broadcasted_iota(jnp.int32, (grp, grp), 0)
    ci = lax.broadcasted_iota(jnp.int32, (grp, grp), 1)
    tri = jnp.where((ri // c == ci // c) & (ci <= ri), 1.0, 0.0).astype(BF16)
    for r in range(tb // grp):
        rem = logf2[r * grp:(r + 1) * grp]
        acc = None
        for _ in range(3):
            piece = rem.astype(BF16)
            rem = rem - piece.astype(F32)
            y = jnp.dot(tri, piece, preferred_element_type=F32)
            acc = y if acc is None else acc + y
        lc_ref[r * grp:(r + 1) * grp, :] = acc

    bi = lax.broadcasted_iota(jnp.int32, (half, half), 0) // HG_DK
    bj = lax.broadcasted_iota(jnp.int32, (half, half), 1) // HG_DK
    bd = jnp.where(bi == bj, 1.0, 0.0)
    bd16 = bd.astype(BF16)

    def chunk(cidx, carry):
        r0 = pl.multiple_of(cidx * c, c)
        qc = q_ref[pl.ds(r0, c), :]
        vc = i_ref[pl.ds(r0, c), :]
        kc = kk_ref[pl.ds(r0, c), :]
        lc = lc_ref[pl.ds(r0, c), :]
        last = lc[c - 1:c, :]
        qt = (qc * jnp.exp2(lc)).astype(BF16)
        o = jnp.concatenate(
            [lax.dot_general(qt[:, g * half:(g + 1) * half], st_ref[g].astype(BF16), CONTRACT_LAST,
                             preferred_element_type=F32) for g in range(2)], axis=1)
        ws = []
        for s in range(c):
            t0 = (s // SUBLANES) * SUBLANES
            dec = jnp.exp2(lc[t0:] - lc[s:s + 1, :])
            if t0:
                ws.append(jnp.zeros((t0, HG_W), F32))
            live = lax.broadcasted_iota(jnp.int32, (c - t0, HG_W), 0) >= s - t0
            ws.append(jnp.where(live, qc[t0:] * kc[s:s + 1, :] * dec, 0.0))
        w = jnp.concatenate(ws, axis=0).astype(BF16)
        att = jnp.concatenate(
            [jnp.dot(w[:, g * half:(g + 1) * half], bd16, preferred_element_type=F32) for g in range(2)], axis=1)
        tiles = [o[t0:t0 + SUBLANES] for t0 in range(0, c, SUBLANES)]
        for s in range(c):
            for ti in range(s // SUBLANES, c // SUBLANES):
                off = s * c + ti * SUBLANES
                tiles[ti] = tiles[ti] + att[off:off + SUBLANES] * vc[s:s + 1, :]
        o_ref[pl.ds(r0, c), :] = jnp.concatenate(tiles, axis=0)
        k2 = (kc * jnp.exp2(last - lc)).astype(BF16)
        vb = vc.astype(BF16)
        dl = jnp.exp2(last)
        for g in range(2):
            upd = lax.dot_general(vb[:, g * half:(g + 1) * half], k2[:, g * half:(g + 1) * half], CONTRACT_FIRST,
                                  preferred_element_type=F32)
            st_ref[g] = st_ref[g] * dl[:, g * half:(g + 1) * half] + bd * upd
        return carry

    lax.fori_loop(0, tb // c, chunk, 0)

    o = o_ref[...]
    o2 = o * o
    ms = jnp.concatenate([_split_dot(o2[:, g * half:(g + 1) * half], bd, 3) for g in range(2)],
                         axis=1) * (1.0 / HG_DK)
    og = g_ref[...]
    o_ref[...] = o * lax.rsqrt(ms + RMS_EPS) * gain_ref[...] * (og * _sigmoid(og))

    @pl.when(t == n_t - 1)
    def _():
        so_ref[...] = st_ref[...]


def _gla_call(h3, lb_logits, gain, s0_bd, layer, tb, c, t_valid):
    b, s, _ = h3.shape
    n_t = s // tb
    n_layers = lb_logits.shape[0]
    col = lambda cidx: pl.BlockSpec((None, tb, COLW), lambda bi, ti, cidx=cidx: (bi, ti, cidx))
    full2 = lambda shape: pl.BlockSpec(shape, lambda bi, ti: (0, 0))
    st_spec = pl.BlockSpec((None, 2, MXU_DIM, MXU_DIM), lambda bi, ti: (bi, 0, 0, 0))
    return pl.pallas_call(
        functools.partial(_gla_kernel, layer=layer, tb=tb, c=c, t_valid=t_valid, n_t=n_t),
        grid=(b, n_t),
        in_specs=[col(C_QH), col(C_FH), col(C_IH), col(C_GH),
                  full2((n_layers, HG_W)), full2((1, HG_W)), st_spec],
        out_specs=[pl.BlockSpec((None, tb, HG_W), lambda bi, ti: (bi, ti, 0)), st_spec],
        out_shape=[jax.ShapeDtypeStruct((b, s, HG_W), F32),
                   jax.ShapeDtypeStruct((b, 2, MXU_DIM, MXU_DIM), F32)],
        scratch_shapes=[pltpu.VMEM((2, MXU_DIM, MXU_DIM), F32),
                        pltpu.VMEM((tb, HG_W), F32),
                        pltpu.VMEM((tb, HG_W), F32)],
        compiler_params=_params(2),
        name="hgrn2",
    )(h3, h3, h3, h3, lb_logits, gain, s0_bd)


def _state_to_bd(s):
    b = s.shape[0]
    st = jnp.swapaxes(s, -1, -2).reshape(b, 2, 4, HG_DK, 1, HG_DK)
    eye = jnp.eye(4, dtype=F32).reshape(1, 1, 4, 1, 4, 1)
    return (st * eye).reshape(b, 2, MXU_DIM, MXU_DIM)


def _state_from_bd(st):
    b = st.shape[0]
    st6 = st.reshape(b, 2, 4, HG_DK, 4, HG_DK)
    blocks = jnp.stack([st6[:, :, hh, :, hh, :] for hh in range(4)], axis=2)
    return jnp.swapaxes(blocks.reshape(b, HG_HEADS, HG_DK, HG_DK), -1, -2)


def _lru_kernel(x_ref, g_ref, cw_ref, cb_ref, wa_ref, ba_ref, wx_ref, bx_ref, lam_ref, h0_ref, buf0_ref,
                y_ref, hl_ref, bo_ref, xp_ref, a_ref, u_ref, hc_ref, *, tb, t_valid, pad, n_t):
    t = pl.program_id(1)
    nprev = CONV_W - 1
    base = SUBLANES

    @pl.when(t == 0)
    def _():
        xp_ref[base - nprev:base, :] = buf0_ref[...]
        hc_ref[...] = h0_ref[...]

    x = x_ref[...]
    xp_ref[base:base + tb, :] = x
    cw = cw_ref[...]
    xc = cb_ref[...] + cw[nprev:nprev + 1, :] * x
    for j in range(nprev):
        xc = xc + cw[j:j + 1, :] * xp_ref[base - nprev + j:base - nprev + j + tb, :]
    bo_ref[...] = xp_ref[base + t_valid - nprev:base + t_valid, :]
    xp_ref[base - nprev:base, :] = xp_ref[base + tb - nprev:base + tb, :]

    xb = xc.astype(BF16)
    r = _sigmoid(jnp.dot(xb, wa_ref[...], preferred_element_type=F32) + ba_ref[...])
    ig = _sigmoid(jnp.dot(xb, wx_ref[...], preferred_element_type=F32) + bx_ref[...])
    z = -lam_ref[...]
    softplus = jnp.maximum(z, 0.0) + jnp.log(1.0 + jnp.exp(-jnp.abs(z)))
    log_a = -LRU_C * r * softplus
    a = jnp.exp(log_a)
    u = jnp.sqrt(jnp.maximum(1.0 - jnp.exp(2.0 * log_a), SQRT_EPS)) * ig * xc
    first = lax.broadcasted_iota(jnp.int32, (tb, LRU_W), 0) == 0
    u = u + jnp.where(first, a * hc_ref[...], 0.0)

    a_ref[0:pad, :] = jnp.ones((pad, LRU_W), F32)
    u_ref[0:pad, :] = jnp.zeros((pad, LRU_W), F32)
    a_ref[pad:pad + tb, :] = a
    u_ref[pad:pad + tb, :] = u
    d = 1
    while d < tb:
        a_cur = a_ref[pad:pad + tb, :]
        u_cur = u_ref[pad:pad + tb, :]
        a_sh = a_ref[pad - d:pad - d + tb, :]
        u_sh = u_ref[pad - d:pad - d + tb, :]
        u_ref[pad:pad + tb, :] = a_cur * u_sh + u_cur
        a_ref[pad:pad + tb, :] = a_cur * a_sh
        d *= 2
    h = u_ref[pad:pad + tb, :]
    y_ref[...] = h * _gelu(g_ref[...])
    hc_ref[...] = h[tb - 1:tb, :]
    hl_ref[...] = h[t_valid - 1:t_valid, :]


def _lru_call(h3, w, h0, buf0, tb, t_valid):
    b, s, _ = h3.shape
    n_t = s // tb
    pad = max(SUBLANES, tb // 2)
    cw, cb, wa_bd, ba, wx_bd, bx, lam = w
    col = lambda cidx: pl.BlockSpec((None, tb, COLW), lambda bi, ti, cidx=cidx: (bi, ti, cidx))
    full2 = lambda shape: pl.BlockSpec(shape, lambda bi, ti: (0, 0))
    vec = full2((1, LRU_W))
    per_b = lambda r: pl.BlockSpec((None, r, LRU_W), lambda bi, ti: (bi, 0, 0))
    return pl.pallas_call(
        functools.partial(_lru_kernel, tb=tb, t_valid=t_valid, pad=pad, n_t=n_t),
        grid=(b, n_t),
        in_specs=[col(C_XL), col(C_GL), full2((CONV_W, LRU_W)), vec, full2((LRU_W, LRU_W)), vec,
                  full2((LRU_W, LRU_W)), vec, vec, per_b(1), per_b(CONV_W - 1)],
        out_specs=[pl.BlockSpec((None, tb, LRU_W), lambda bi, ti: (bi, ti, 0)), per_b(1), per_b(CONV_W - 1)],
        out_shape=[jax.ShapeDtypeStruct((b, s, LRU_W), F32),
                   jax.ShapeDtypeStruct((b, 1, LRU_W), F32),
                   jax.ShapeDtypeStruct((b, CONV_W - 1, LRU_W), F32)],
        scratch_shapes=[pltpu.VMEM((SUBLANES + tb, LRU_W), F32),
                        pltpu.VMEM((pad + tb, LRU_W), F32),
                        pltpu.VMEM((pad + tb, LRU_W), F32),
                        pltpu.VMEM((1, LRU_W), F32)],
        compiler_params=_params(2),
        name="rglru",
    )(h3, h3, cw, cb, wa_bd, ba, wx_bd, bx, lam, h0, buf0)


def _block_diag(w):
    eye = jnp.eye(LRU_BLOCKS, dtype=w.dtype).reshape(LRU_BLOCKS, 1, LRU_BLOCKS, 1)
    return (w[:, :, None, :] * eye).reshape(LRU_W, LRU_W)


def _merge_kernel(x_ref, oa_ref, oh_ref, ol_ref, g0a, g0b, g1a, g1b, g2a, g2b,
                  wa_ref, wh_ref, wl_ref, wo_ref, lg_ref, lb_ref, y_ref, *, alpha):
    def branch(o_ref, w_ref, ga, gb):
        y = jnp.dot(o_ref[...].astype(BF16), w_ref[...], preferred_element_type=F32)
        gate = jnp.concatenate([ga[...], gb[...]], axis=1)
        return _sigmoid(gate) * y

    merged = branch(oa_ref, wa_ref, g0a, g0b) + branch(oh_ref, wh_ref, g1a, g1b) + branch(ol_ref, wl_ref, g2a, g2b)
    mix = jnp.dot(merged.astype(BF16), wo_ref[...], preferred_element_type=F32)
    y_ref[...] = _layer_norm(alpha * x_ref[...] + mix, lg_ref[...], lb_ref[...])


def _merge_call(x2, h2, o_att, o_hg, o_lru, w, tm, alpha):
    m = x2.shape[0]
    w_att, w_hg, w_lru, w_o, ln_g, ln_b = w
    row = lambda width: pl.BlockSpec((tm, width), lambda i: (i, 0))
    col = lambda cidx: pl.BlockSpec((tm, COLW), lambda i, cidx=cidx: (i, cidx))
    full = lambda shape: pl.BlockSpec(shape, lambda i: (0, 0))
    return pl.pallas_call(
        functools.partial(_merge_kernel, alpha=alpha),
        grid=(m // tm,),
        in_specs=[row(D_MODEL), row(ATT_W), row(HG_W), row(LRU_W)]
                 + [col(C_GM + j) for j in range(6)]
                 + [full((ATT_W, D_MODEL)), full((HG_W, D_MODEL)), full((LRU_W, D_MODEL)),
                    full((D_MODEL, D_MODEL)), full((1, D_MODEL)), full((1, D_MODEL))],
        out_specs=row(D_MODEL),
        out_shape=jax.ShapeDtypeStruct((m, D_MODEL), F32),
        compiler_params=_params(1),
        name="merge_ln",
    )(x2, o_att, o_hg, o_lru, h2, h2, h2, h2, h2, h2, w_att, w_hg, w_lru, w_o, ln_g, ln_b)


def _ffn_kernel(x_ref, wup_ref, cw_ref, cb_ref, wdn_ref, lg_ref, lb_ref, p2_ref, p1_ref,
                y_ref, tail_ref, carry_ref, *, tm, ts, fc, alpha, per_row_prev, n_t):
    i = pl.program_id(0)
    x = x_ref[...]
    xb = x.astype(BF16)
    tpos = lax.broadcasted_iota(jnp.int32, (tm, fc), 0) % ts
    cw = cw_ref[...]
    cb = cb_ref[...]

    if not per_row_prev:
        @pl.when(i % n_t == 0)
        def _():
            carry_ref[...] = p2_ref[...]

    acc = jnp.zeros((tm, D_MODEL), F32)
    for cidx in range(D_FF // fc):
        lo, hi = cidx * fc, (cidx + 1) * fc
        u = jnp.dot(xb, wup_ref[:, lo:hi], preferred_element_type=F32)
        val = jnp.dot(xb, wup_ref[:, D_FF + lo:D_FF + hi], preferred_element_type=F32)
        if per_row_prev:
            prev2 = p2_ref[:, lo:hi]
            prev1 = p1_ref[:, lo:hi]
        else:
            cm2 = carry_ref[0:1, lo:hi]
            cm1 = carry_ref[1:2, lo:hi]
            prev2 = jnp.where(tpos == 0, cm2, cm1)
            prev1 = jnp.broadcast_to(cm1, (tm, fc))
        u1 = jnp.where(tpos >= 1, pltpu.roll(u, 1, 0), prev1)
        u2 = jnp.where(tpos >= 2, pltpu.roll(u, 2, 0), prev2)
        uc = cb[:, lo:hi] + cw[0:1, lo:hi] * u2 + cw[1:2, lo:hi] * u1 + cw[2:3, lo:hi] * u
        if per_row_prev:
            tail_ref[:, lo:hi] = u
        else:
            carry_ref[:, lo:hi] = u[tm - 2:tm, :]
            tail_ref[:, lo:hi] = u[tm - SUBLANES:tm, :]
        gated = (_gelu(uc) * val).astype(BF16)
        acc = acc + jnp.dot(gated, wdn_ref[lo:hi, :], preferred_element_type=F32)
    y_ref[...] = _layer_norm(alpha * x + acc, lg_ref[...], lb_ref[...])


def _ffn_call(x2, w, prev2, prev1, tm, ts, n_seq, alpha, per_row_prev):
    m = x2.shape[0]
    w_up, cw, cb, w_dn, ln_g, ln_b = w
    n_t = (m // n_seq) // tm if not per_row_prev else 1
    fc = 1024
    row = pl.BlockSpec((tm, D_MODEL), lambda i: (i, 0))
    full = lambda shape: pl.BlockSpec(shape, lambda i: (0, 0))
    resident = lambda shape: pl.BlockSpec(shape, lambda i: (0, 0), pipeline_mode=pl.Buffered(1))
    if per_row_prev:
        p2_spec = pl.BlockSpec((tm, D_FF), lambda i: (i, 0))
        p1_spec = pl.BlockSpec((tm, D_FF), lambda i: (i, 0))
        tail_spec = pl.BlockSpec((tm, D_FF), lambda i: (i, 0))
        tail_shape = jax.ShapeDtypeStruct((m, D_FF), F32)
    else:
        p2_spec = pl.BlockSpec((None, FFN_CONV_W - 1, D_FF), lambda i: (i // n_t, 0, 0))
        p1_spec = pl.BlockSpec((None, FFN_CONV_W - 1, D_FF), lambda i: (i // n_t, 0, 0))
        tail_spec = pl.BlockSpec((None, SUBLANES, D_FF), lambda i: (i // n_t, 0, 0))
        tail_shape = jax.ShapeDtypeStruct((n_seq, SUBLANES, D_FF), F32)
    return pl.pallas_call(
        functools.partial(_ffn_kernel, tm=tm, ts=ts, fc=fc, alpha=alpha, per_row_prev=per_row_prev, n_t=n_t),
        grid=(m // tm,),
        in_specs=[row, resident((D_MODEL, 2 * D_FF)), full((FFN_CONV_W, D_FF)), full((1, D_FF)),
                  resident((D_FF, D_MODEL)), full((1, D_MODEL)), full((1, D_MODEL)), p2_spec, p1_spec],
        out_specs=[row, tail_spec],
        out_shape=[jax.ShapeDtypeStruct((m, D_MODEL), F32), tail_shape],
        scratch_shapes=[pltpu.VMEM((FFN_CONV_W - 1, D_FF), F32)],
        compiler_params=_params(1),
        name="ffn_ln",
    )(x2, w_up, cw, cb, w_dn, ln_g, ln_b, prev2, prev1)


def _trunk_layer(x3, tabs, attn_fn, hg_s0, lru_h0, lru_buf0, ffn_buf0, wl, layer, cfg):
    b, s, _ = x3.shape
    m = b * s
    x2 = x3.reshape(m, D_MODEL)
    h2 = _matmul(x2, wl["w_in"], cfg["bm"], cfg["bn"])
    h3 = h2.reshape(b, s, D_IN)
    rope_out = _rope_call(h3, tabs, cfg["rope_tm"], cfg["prompt"])
    k_rot, v_f32 = rope_out[1], rope_out[2]
    o_att = attn_fn(*rope_out)
    o_hg, hg_st = _gla_call(h3, wl["hg_lb_logits"], wl["hg_gain"], hg_s0, layer,
                            cfg["seq_tb"], cfg["gla_c"], cfg["t_valid"])
    o_lru, lru_h, lru_buf = _lru_call(h3, wl["lru"], lru_h0, lru_buf0, cfg["seq_tb"], cfg["t_valid"])
    x1 = _merge_call(x2, h2, o_att.reshape(m, ATT_W), o_hg.reshape(m, HG_W), o_lru.reshape(m, LRU_W),
                     wl["merge"], cfg["tok_tm"], cfg["alpha"])
    if cfg["prompt"]:
        x_out, tail = _ffn_call(x1, wl["ffn"], ffn_buf0, ffn_buf0, cfg["tok_tm"], cfg["tok_tm"], b,
                                cfg["alpha"], False)
        ffn_buf = tail[:, SUBLANES - (FFN_CONV_W - 1):, :]
    else:
        tv = cfg["t_valid"]
        zrow = lambda n: jnp.zeros((b, n, D_FF), F32)
        prev2 = jnp.concatenate([ffn_buf0, zrow(s - 2)], axis=1).reshape(m, D_FF)
        prev1 = jnp.concatenate([ffn_buf0[:, 1:2], zrow(s - 1)], axis=1).reshape(m, D_FF)
        x_out, u_all = _ffn_call(x1, wl["ffn"], prev2, prev1, m, s, b, cfg["alpha"], True)
        ffn_buf = u_all.reshape(b, s, D_FF)[:, tv - (FFN_CONV_W - 1):tv]
    return (x_out.reshape(b, s, D_MODEL), k_rot, v_f32, hg_st, lru_h.reshape(b, LRU_W), lru_buf, ffn_buf)


def kernel(x_prompt, x_sample, cache_k, cache_v, page_table, state_hgrn, state_lru_h, state_lru_conv, state_ffn_conv, w_in, hg_lb_logits, hg_gain, lru_conv_w, lru_conv_b, lru_wa, lru_ba, lru_wx, lru_bx, lru_lambda, w_br_att, w_br_hg, w_br_lru, w_o, ln1_g, ln1_b, ffn_w_up, ffn_conv_w, ffn_conv_b, ffn_w_down, ln2_g, ln2_b):
    bsz, seq, _ = x_prompt.shape
    dbs, n_tok, _ = x_sample.shape
    depth = w_in.shape[0]
    page = cache_k.shape[2]
    past = page_table.shape[1] * page
    assert seq % MOBA_BLOCK == 0 and past % MOBA_BLOCK == 0
    assert CONV_W - 1 <= n_tok <= SUBLANES
    alpha = (2.0 * depth) ** 0.25
    tpad = SUBLANES
    n_blk = seq // MOBA_BLOCK
    nbp = -(-n_blk // SUBLANES) * SUBLANES

    tabs_p = _rope_tables(jnp.arange(seq))
    tabs_s = _rope_tables(past + jnp.arange(tpad))

    seq_tb = min(512, seq)
    assert seq % seq_tb == 0 and (bsz * seq) % min(1024, bsz * seq) == 0
    cfg_p = dict(prompt=True, bm=min(1024, bsz * seq), bn=1536, rope_tm=min(512, seq), seq_tb=seq_tb,
                 gla_c=16, t_valid=seq_tb, tok_tm=min(512, seq), alpha=alpha)
    cfg_s = dict(prompt=False, bm=dbs * tpad, bn=1536, rope_tm=tpad, seq_tb=tpad,
                 gla_c=tpad, t_valid=n_tok, tok_tm=dbs * tpad, alpha=alpha)

    xp = x_prompt
    xs = jnp.pad(x_sample, ((0, 0), (0, tpad - n_tok), (0, 0)))
    hg0_p = jnp.zeros((bsz, 2, MXU_DIM, MXU_DIM), F32)
    lh0_p = jnp.zeros((bsz, 1, LRU_W), F32)
    lb0_p = jnp.zeros((bsz, CONV_W - 1, LRU_W), F32)
    fb0_p = jnp.zeros((bsz, FFN_CONV_W - 1, D_FF), F32)

    def attn_p(q_rot, k_rot, v_f32, k_bf, v_t, ksum):
        ks = jnp.pad(ksum.reshape(bsz, n_blk, ATT_W), ((0, 0), (0, nbp - n_blk), (0, 0)))
        return _attn_prompt_call(q_rot, k_bf, v_t, ks)

    outs_p = [[] for _ in range(6)]
    outs_s = [[] for _ in range(6)]
    row = lambda a: a.reshape(1, -1)
    for l in range(depth):
        wl = dict(
            w_in=w_in[l].astype(BF16),
            hg_lb_logits=hg_lb_logits,
            hg_gain=row(hg_gain[l]),
            lru=(lru_conv_w[l], row(lru_conv_b[l]), _block_diag(lru_wa[l]).astype(BF16), row(lru_ba[l]),
                 _block_diag(lru_wx[l]).astype(BF16), row(lru_bx[l]), row(lru_lambda[l])),
            merge=(w_br_att[l].astype(BF16), w_br_hg[l].astype(BF16), w_br_lru[l].astype(BF16),
                   w_o[l].astype(BF16), row(ln1_g[l]), row(ln1_b[l])),
            ffn=(ffn_w_up[l].astype(BF16), ffn_conv_w[l], row(ffn_conv_b[l]), ffn_w_down[l].astype(BF16),
                 row(ln2_g[l]), row(ln2_b[l])),
        )

        def attn_s(q_rot, k_rot, v_f32, l=l):
            heads4 = lambda a: a.reshape(dbs, tpad, ATT_HEADS, HEAD_DIM)
            o = _attn_sample_call(page_table, heads4(q_rot), heads4(k_rot), heads4(v_f32),
                                  cache_k, cache_v, l, n_tok)
            o = o.reshape(dbs, n_tok, ATT_W)
            return jnp.pad(o, ((0, 0), (0, tpad - n_tok), (0, 0)))

        xp, *new_p = _trunk_layer(xp, tabs_p, attn_p, hg0_p, lh0_p, lb0_p, fb0_p, wl, l, cfg_p)
        xs, *new_s = _trunk_layer(xs, tabs_s, attn_s, _state_to_bd(state_hgrn[:, l]),
                                  state_lru_h[:, l].reshape(dbs, 1, LRU_W), state_lru_conv[:, l],
                                  state_ffn_conv[:, l], wl, l, cfg_s)
        for lst, a in zip(outs_p, new_p):
            lst.append(a)
        for lst, a in zip(outs_s, new_s):
            lst.append(a)

    def heads(a, t):
        return a.reshape(a.shape[0], a.shape[1], t, ATT_HEADS, HEAD_DIM)

    k_p = heads(jnp.stack(outs_p[0], axis=1), seq)
    v_p = heads(jnp.stack(outs_p[1], axis=1), seq)
    k_s = heads(jnp.stack([a[:, :n_tok] for a in outs_s[0]], axis=1), n_tok)
    v_s = heads(jnp.stack([a[:, :n_tok] for a in outs_s[1]], axis=1), n_tok)
    hg_p = jnp.stack([_state_from_bd(a) for a in outs_p[2]], axis=1)
    hg_s = jnp.stack([_state_from_bd(a) for a in outs_s[2]], axis=1)
    lh_p = jnp.stack(outs_p[3], axis=1)
    lh_s = jnp.stack(outs_s[3], axis=1)
    lc_p = jnp.stack(outs_p[4], axis=1)
    lc_s = jnp.stack(outs_s[4], axis=1)
    fc_p = jnp.stack(outs_p[5], axis=1)
    fc_s = jnp.stack(outs_s[5], axis=1)
    return (xp, xs[:, :n_tok], k_p, v_p, k_s, v_s, hg_p, hg_s, lh_p, lh_s, lc_p, lc_s, fc_p, fc_s)
```

```python
import functools
import math

import jax
import jax.numpy as jnp
from jax import lax
from jax.experimental import pallas as pl
from jax.experimental.pallas import tpu as pltpu

F32 = jnp.float32
BF16 = jnp.bfloat16

D_MODEL = 1024
ATT_HEADS = 8
HEAD_DIM = 64
ATT_W = ATT_HEADS * HEAD_DIM
MOBA_BLOCK = 256
MOBA_TOPK = 3
ROPE_THETA = 500000.0
ROPE_DIM = HEAD_DIM // 4
HG_HEADS = 8
HG_DK = 64
HG_W = HG_HEADS * HG_DK
LRU_W = 512
LRU_BLOCKS = 8
LRU_BD = LRU_W // LRU_BLOCKS
LRU_C = 8.0
CONV_W = 4
D_FF = 3 * D_MODEL
FFN_CONV_W = 3
N_BRANCH = 3
LN_EPS = 1e-5
RMS_EPS = 1e-6
SQRT_EPS = 1e-12
NEG_INF = -1e30
D_IN = 3 * ATT_W + 4 * HG_W + 2 * LRU_W + N_BRANCH * D_MODEL
LOG2E = math.log2(math.e)
ATTN_GROUP = 2
SAMPLE_BLOCKS_PER_STEP = 2

COLW = 512
C_QA, C_KA, C_VA, C_QH, C_FH, C_IH, C_GH, C_XL, C_GL, C_GM = 0, 1, 2, 3, 4, 5, 6, 7, 8, 9

LANES = 128
SUBLANES = 8
MXU_DIM = 256
VMEM_LIMIT = 56 * 1024 * 1024

HIGHEST = lax.Precision.HIGHEST
CONTRACT_LAST = (((1,), (1,)), ((), ()))
CONTRACT_FIRST = (((0,), (0,)), ((), ()))


def _params(n_grid):
    return pltpu.CompilerParams(dimension_semantics=("arbitrary",) * n_grid,
                                vmem_limit_bytes=VMEM_LIMIT)


def _sigmoid(x):
    return 1.0 / (1.0 + jnp.exp(-x))


def _gelu(x):
    c = math.sqrt(2.0 / math.pi)
    return x * (0.5 * (1.0 + jnp.tanh(c * (x + 0.044715 * (x * x * x)))))


def _layer_norm(y, g, b):
    mu = jnp.mean(y, axis=-1, keepdims=True)
    d = y - mu
    var = jnp.mean(d * d, axis=-1, keepdims=True)
    return d * lax.rsqrt(var + LN_EPS) * g + b


def _split_dot(x, w01, parts):
    w = w01.astype(BF16)
    acc = None
    rem = x
    for _ in range(parts):
        piece = rem.astype(BF16)
        rem = rem - piece.astype(F32)
        y = jnp.dot(piece, w, preferred_element_type=F32)
        acc = y if acc is None else acc + y
    return acc


def _top_k_mask(gate, valid, axis):
    idx_all = lax.broadcasted_iota(jnp.int32, gate.shape, axis).astype(F32)
    g = jnp.where(valid, gate, -jnp.inf)
    sel = jnp.zeros(gate.shape, F32)
    for _ in range(MOBA_TOPK):
        mx = jnp.max(g, axis=axis, keepdims=True)
        idx = jnp.min(jnp.where(g == mx, idx_all, float(gate.shape[axis])), axis=axis, keepdims=True)
        pick = idx_all == idx
        sel = jnp.where(pick, 1.0, sel)
        g = jnp.where(pick, -jnp.inf, g)
    return jnp.where(valid, sel, 0.0)


def _mm_kernel(x_ref, w_ref, o_ref):
    o_ref[...] = jnp.dot(x_ref[...].astype(BF16), w_ref[...], preferred_element_type=F32)


def _matmul(x, w, bm, bn):
    m, k = x.shape
    n = w.shape[1]
    assert m % bm == 0 and n % bn == 0
    return pl.pallas_call(
        _mm_kernel,
        grid=(m // bm, n // bn),
        in_specs=[pl.BlockSpec((bm, k), lambda i, j: (i, 0)),
                  pl.BlockSpec((k, bn), lambda i, j: (0, j))],
        out_specs=pl.BlockSpec((bm, bn), lambda i, j: (i, j)),
        out_shape=jax.ShapeDtypeStruct((m, n), F32),
        compiler_params=_params(2),
        name="in_proj",
    )(x, w)


def _rope_kernel(q_ref, k_ref, v_ref, c_ref, sa_ref, sb_ref, qo_ref, ko_ref, vo_ref, *extra, n_blk):
    c = c_ref[...]
    sa = sa_ref[...]
    sb = sb_ref[...]

    def rot(x):
        outs = []
        for j in range(ATT_W // LANES):
            xs = x[:, LANES * j:LANES * (j + 1)]
            outs.append(xs * c + pltpu.roll(xs, ROPE_DIM // 2, 1) * sa
                        + pltpu.roll(xs, LANES - ROPE_DIM // 2, 1) * sb)
        return jnp.concatenate(outs, axis=1)

    q = rot(q_ref[...])
    k = rot(k_ref[...])
    v = v_ref[...]
    qo_ref[...] = q
    if n_blk:
        kb_ref, vt_ref, ks_ref = extra
        kb_ref[...] = k.astype(BF16)
        v_t = v.T
        ko_ref[...] = k.T
        vo_ref[...] = v_t
        for r in range(n_blk):
            rows = slice(MOBA_BLOCK * r, MOBA_BLOCK * (r + 1))
            ks_ref[r] = jnp.sum(k[rows], axis=0, keepdims=True)
            vt_ref[r] = v_t[:, rows].astype(BF16)
    else:
        ko_ref[...] = k
        vo_ref[...] = v


def _rope_call(h3, tabs, tm, prompt):
    b, s, _ = h3.shape
    nt = s // tm
    n_blk = tm // MOBA_BLOCK if prompt else 0
    col = lambda cidx: pl.BlockSpec((None, tm, COLW), lambda bi, ti, cidx=cidx: (bi, ti, cidx))
    tab = pl.BlockSpec((tm, LANES), lambda bi, ti: (ti, 0))
    row = pl.BlockSpec((None, tm, ATT_W), lambda bi, ti: (bi, ti, 0))
    out_specs = [row, row, row]
    out_shape = [jax.ShapeDtypeStruct((b, s, ATT_W), F32)] * 3
    if prompt:
        row_t = pl.BlockSpec((None, ATT_W, tm), lambda bi, ti: (bi, 0, ti))
        out_specs = [row, row_t, row_t]
        out_shape = [jax.ShapeDtypeStruct((b, s, ATT_W), F32)] + [jax.ShapeDtypeStruct((b, ATT_W, s), F32)] * 2
        out_specs += [row,
                      pl.BlockSpec((None, n_blk, ATT_W, MOBA_BLOCK), lambda bi, ti: (bi, ti, 0, 0)),
                      pl.BlockSpec((None, n_blk, 1, ATT_W), lambda bi, ti: (bi, ti, 0, 0))]
        out_shape += [jax.ShapeDtypeStruct((b, s, ATT_W), BF16),
                      jax.ShapeDtypeStruct((b, s // MOBA_BLOCK, ATT_W, MOBA_BLOCK), BF16),
                      jax.ShapeDtypeStruct((b, s // MOBA_BLOCK, 1, ATT_W), F32)]
    return pl.pallas_call(
        functools.partial(_rope_kernel, n_blk=n_blk),
        grid=(b, nt),
        in_specs=[col(C_QA), col(C_KA), col(C_VA), tab, tab, tab],
        out_specs=out_specs,
        out_shape=out_shape,
        compiler_params=_params(2),
        name="rope",
    )(h3, h3, h3, *tabs)


def _rope_tables(pos):
    half = ROPE_DIM // 2
    inv = ROPE_THETA ** (-jnp.arange(half, dtype=F32) / half)
    ang = pos.astype(F32)[:, None] * inv[None, :]
    cos, sin = jnp.cos(ang), jnp.sin(ang)
    lh = jnp.arange(LANES) % HEAD_DIM
    fi = lh % half
    c = jnp.where(lh[None, :] < ROPE_DIM, cos[:, fi], 1.0)
    sa = jnp.where((lh[None, :] >= half) & (lh[None, :] < ROPE_DIM), sin[:, fi], 0.0)
    sb = jnp.where(lh[None, :] < half, -sin[:, fi], 0.0)
    return c.astype(F32), sa.astype(F32), sb.astype(F32)


def _attn_prompt_kernel(q_ref, k_ref, vt_ref, ks_ref, o_ref, bias_ref, sa_ref, sb_ref, *, tq, nbp, nb, n_blk):
    i = pl.program_id(2)
    qscale = HEAD_DIM ** -0.5 * LOG2E
    q = q_ref[...]
    kmean = ks_ref[...] * (1.0 / MOBA_BLOCK)
    lane = lax.broadcasted_iota(jnp.int32, (tq, LANES), 1)
    valid = lax.broadcasted_iota(jnp.int32, (nbp, tq), 0) < i
    causal = (lax.broadcasted_iota(jnp.int32, (tq, tq), 0) <= lax.broadcasted_iota(jnp.int32, (tq, tq), 1))
    own0 = pl.multiple_of(i * tq, tq)
    k_own = k_ref[pl.ds(own0, tq), :]
    vt_own = vt_ref[i]

    qs = []
    state = []
    for hh in range(2):
        in_head = (lane >= hh * HEAD_DIM) & (lane < (hh + 1) * HEAD_DIM)
        qh = jnp.where(in_head, q, 0.0)
        gate_t = lax.dot_general(kmean, qh, CONTRACT_LAST, precision=HIGHEST, preferred_element_type=F32)
        sel = _top_k_mask(gate_t, valid, 0)
        bias_ref[hh, 0:nbp, :] = jnp.where(sel > 0.0, 0.0, NEG_INF)
        qsh = (qh * qscale).astype(BF16)
        qs.append(qsh)
        s = lax.dot_general(k_own, qsh, CONTRACT_LAST, preferred_element_type=F32)
        s = jnp.where(causal, s, NEG_INF)
        m = jnp.max(s, axis=0, keepdims=True)
        p = jnp.exp2(s - m)
        l = jnp.sum(p, axis=0, keepdims=True)
        acc = jnp.dot(vt_own[hh * HEAD_DIM:(hh + 1) * HEAD_DIM, :], p.astype(BF16), preferred_element_type=F32)
        state += [m, l, acc]

    neg_rows = jnp.full((SUBLANES, tq), NEG_INF, F32)
    bias_ref[0, nbp:nbp + SUBLANES, :] = neg_rows
    bias_ref[1, nbp:nbp + SUBLANES, :] = neg_rows

    def form_scores(jg, s_ref):
        gmax = [None, None]
        for r in range(nb):
            j = jg * nb + r
            jk = jnp.minimum(j, n_blk - 1)
            jb = jnp.minimum(j, nbp)
            kj = k_ref[pl.ds(pl.multiple_of(jk * tq, tq), tq), :]
            for hh in range(2):
                s = (lax.dot_general(kj, qs[hh], CONTRACT_LAST, preferred_element_type=F32)
                     + bias_ref[hh, pl.ds(jb, 1), :])
                s_ref[r * 2 + hh] = s
                mx = jnp.max(s, axis=0, keepdims=True)
                gmax[hh] = mx if gmax[hh] is None else jnp.maximum(gmax[hh], mx)
        return gmax

    def softmax_pv(jg, s_ref, gmax, st):
        out = []
        for hh in range(2):
            m, l, acc = st[3 * hh:3 * hh + 3]
            m_new = jnp.maximum(m, gmax[hh])
            alpha = jnp.exp2(m - m_new)
            l = alpha * l
            acc = alpha * acc
            for r in range(nb):
                vtj = vt_ref[jnp.minimum(jg * nb + r, n_blk - 1)]
                p = jnp.exp2(s_ref[r * 2 + hh] - m_new)
                l = l + jnp.sum(p, axis=0, keepdims=True)
                acc = acc + jnp.dot(vtj[hh * HEAD_DIM:(hh + 1) * HEAD_DIM, :], p.astype(BF16),
                                    preferred_element_type=F32)
            out += [m_new, l, acc]
        return out

    def body(it, carry):
        st, gmax_a = list(carry[:6]), list(carry[6:])
        gmax_b = form_scores(2 * it + 1, sb_ref)
        st = softmax_pv(2 * it, sa_ref, gmax_a, st)
        gmax_a = form_scores(2 * it + 2, sa_ref)
        st = softmax_pv(2 * it + 1, sb_ref, gmax_b, st)
        return tuple(st) + tuple(gmax_a)

    n_groups = (i + nb - 1) // nb
    res = lax.fori_loop(0, (n_groups + 1) // 2, body, tuple(state) + tuple(form_scores(0, sa_ref)))
    st = res[:6]
    o_t = jnp.concatenate([st[2] / st[1], st[5] / st[4]], axis=0)
    o_ref[...] = o_t.T


def _attn_prompt_call(q, kb, vt, ksum):
    b, s, _ = q.shape
    tq = MOBA_BLOCK
    n_blk = s // tq
    nbp = ksum.shape[1]
    nb = math.gcd(n_blk, ATTN_GROUP)
    n_pairs = ATT_W // LANES
    return pl.pallas_call(
        functools.partial(_attn_prompt_kernel, tq=tq, nbp=nbp, nb=nb, n_blk=n_blk),
        grid=(b, n_pairs, n_blk),
        in_specs=[pl.BlockSpec((None, tq, LANES), lambda bi, hp, i: (bi, i, hp)),
                  pl.BlockSpec((None, s, LANES), lambda bi, hp, i: (bi, 0, hp)),
                  pl.BlockSpec((None, n_blk, LANES, tq), lambda bi, hp, i: (bi, 0, hp, 0)),
                  pl.BlockSpec((None, nbp, LANES), lambda bi, hp, i: (bi, 0, hp))],
        out_specs=pl.BlockSpec((None, tq, LANES), lambda bi, hp, i: (bi, i, hp)),
        out_shape=jax.ShapeDtypeStruct((b, s, ATT_W), F32),
        scratch_shapes=[pltpu.VMEM((2, nbp + SUBLANES, tq), F32),
                        pltpu.VMEM((2 * nb, tq, tq), F32),
                        pltpu.VMEM((2 * nb, tq, tq), F32)],
        compiler_params=_params(3),
        name="attn_prompt",
    )(q, kb, vt, ksum)


def _attn_sample_kernel(pt_ref, q_ref, kn_ref, vn_ref, *refs, n_tok, n_blocks, tpad, page, bps):
    del pt_ref
    kp = refs[:2 * bps]
    vp = refs[2 * bps:4 * bps]
    o_ref, m_ref, l_ref, g_ref, oall_ref = refs[4 * bps:]
    n = pl.program_id(1)
    n_steps = n_blocks // bps
    rows = n_tok * ATT_HEADS
    scale = HEAD_DIM ** -0.5
    q = q_ref[...]
    qrep = jnp.concatenate([jnp.broadcast_to(q[t:t + 1], (ATT_HEADS, ATT_W)) for t in range(n_tok)], axis=0)
    head_mask = (lax.broadcasted_iota(jnp.int32, (rows, ATT_W), 0) % ATT_HEADS
                 == lax.broadcasted_iota(jnp.int32, (rows, ATT_W), 1) // HEAD_DIM)
    qs = (jnp.where(head_mask, qrep, 0.0) * scale).astype(BF16)
    lane = lax.broadcasted_iota(jnp.int32, (rows, LANES), 1)

    @pl.when(n == 0)
    def _():
        m_ref[...] = jnp.full(m_ref.shape, NEG_INF, F32)
        l_ref[...] = jnp.zeros(l_ref.shape, F32)
        g_ref[...] = jnp.zeros(g_ref.shape, F32)

    for r in range(bps):
        blk = n * bps + r
        k_t = jnp.concatenate([kp[2 * r][...].reshape(ATT_W, page), kp[2 * r + 1][...].reshape(ATT_W, page)],
                              axis=1).astype(BF16)
        v_t = jnp.concatenate([vp[2 * r][...].reshape(ATT_W, page), vp[2 * r + 1][...].reshape(ATT_W, page)],
                              axis=1).astype(BF16)
        s = jnp.dot(qs, k_t, preferred_element_type=F32)
        g_n = jnp.sum(s, axis=1, keepdims=True) * (1.0 / (scale * MOBA_BLOCK))
        m_n = jnp.max(s, axis=1, keepdims=True)
        p = jnp.exp(s - m_n)
        l_n = jnp.sum(p, axis=1, keepdims=True)
        oall_ref[blk] = lax.dot_general(p.astype(BF16), v_t, CONTRACT_LAST, preferred_element_type=F32)
        m_ref[...] = jnp.where(lane == blk, m_n, m_ref[...])
        l_ref[...] = jnp.where(lane == blk, l_n, l_ref[...])
        g_ref[...] = jnp.where(lane == blk, g_n, g_ref[...])

    @pl.when(n == n_steps - 1)
    def _():
        valid = lane < n_blocks
        sel = _top_k_mask(g_ref[...], valid, 1) > 0.0
        m_all = m_ref[...]
        l_all = l_ref[...]
        kn = kn_ref[...].astype(BF16)
        vn = vn_ref[...].astype(BF16)
        s_own = lax.dot_general(qs, kn, CONTRACT_LAST, preferred_element_type=F32)
        tq = lax.broadcasted_iota(jnp.int32, (rows, tpad), 0) // ATT_HEADS
        tk = lax.broadcasted_iota(jnp.int32, (rows, tpad), 1)
        s_own = jnp.where((tk <= tq) & (tk < n_tok), s_own, NEG_INF)
        m_own = jnp.max(s_own, axis=1, keepdims=True)
        m_fin = jnp.maximum(m_own, jnp.max(jnp.where(sel, m_all, NEG_INF), axis=1, keepdims=True))
        p_own = jnp.exp(s_own - m_fin)
        w = jnp.where(sel, jnp.exp(m_all - m_fin), 0.0)
        l_fin = jnp.sum(p_own, axis=1, keepdims=True) + jnp.sum(w * l_all, axis=1, keepdims=True)
        o = jnp.dot(p_own.astype(BF16), vn, preferred_element_type=F32)
        for nb in range(n_blocks):
            o = o + w[:, nb:nb + 1] * oall_ref[nb]
        o = jnp.where(head_mask, o / l_fin, 0.0)
        o_tok = jnp.sum(o.reshape(n_tok, ATT_HEADS, ATT_W), axis=1)
        o_ref[...] = jnp.concatenate([o_tok, jnp.zeros((tpad - n_tok, ATT_W), F32)], axis=0)


def _attn_sample_call(page_table, q, k_new, v_new, cache_kt, cache_vt, layer, n_tok):
    db, tpad, _ = q.shape
    n_pages = page_table.shape[1]
    page = cache_kt.shape[-1]
    assert MOBA_BLOCK == 2 * page, "two cache pages per MoBA block expected"
    n_blocks = n_pages // 2
    assert n_blocks <= LANES
    bps = math.gcd(n_blocks, SAMPLE_BLOCKS_PER_STEP)
    rows = n_tok * ATT_HEADS
    cpage = lambda off: pl.BlockSpec((None, None, ATT_HEADS, HEAD_DIM, page),
                                     lambda bi, n, pt, off=off: (pt[bi, 2 * bps * n + off], layer, 0, 0, 0))
    pages = [cpage(off) for off in range(2 * bps)]
    tok = pl.BlockSpec((None, tpad, ATT_W), lambda bi, n, pt: (bi, 0, 0))
    grid_spec = pltpu.PrefetchScalarGridSpec(
        num_scalar_prefetch=1,
        grid=(db, n_blocks // bps),
        in_specs=[tok, tok, tok] + pages + pages,
        out_specs=tok,
        scratch_shapes=[pltpu.VMEM((rows, LANES), F32),
                        pltpu.VMEM((rows, LANES), F32),
                        pltpu.VMEM((rows, LANES), F32),
                        pltpu.VMEM((n_blocks, rows, ATT_W), F32)],
    )
    return pl.pallas_call(
        functools.partial(_attn_sample_kernel, n_tok=n_tok, n_blocks=n_blocks, tpad=tpad, page=page, bps=bps),
        grid_spec=grid_spec,
        out_shape=jax.ShapeDtypeStruct((db, tpad, ATT_W), F32),
        compiler_params=_params(2),
        name="attn_sample",
    )(page_table, q, k_new, v_new, *([cache_kt] * (2 * bps)), *([cache_vt] * (2 * bps)))


def _gla_kernel(q_ref, f_ref, i_ref, g_ref, lbl_ref, gain_ref, s0_ref, o_ref, so_ref,
                st_ref, lc_ref, kk_ref, *, bb, tb, c, **kw):
    seqs = [_gla_sequence(q_ref.at[b], f_ref.at[b], i_ref.at[b], g_ref.at[b], lbl_ref, gain_ref, s0_ref.at[b],
                          o_ref.at[b], so_ref.at[b], st_ref.at[b], lc_ref.at[b], kk_ref.at[b], tb=tb, c=c, **kw)
            for b in range(bb)]

    def chunk(cidx, carry):
        for chunk_step, _ in seqs:
            chunk_step(cidx)
        return carry

    lax.fori_loop(0, tb // c, chunk, 0)
    for _, finish in seqs:
        finish()


def _gla_sequence(q_ref, f_ref, i_ref, g_ref, lbl_ref, gain_ref, s0_ref, o_ref, so_ref,
                  st_ref, lc_ref, kk_ref, *, layer, tb, c, t_valid, n_t):
    t = pl.program_id(1)
    half = MXU_DIM

    @pl.when(t == 0)
    def _():
        st_ref[...] = s0_ref[...]

    lg = lbl_ref[...]
    e = jnp.exp(lg - jnp.max(lg, axis=0, keepdims=True))
    den = jnp.sum(e, axis=0, keepdims=True)
    lb = jnp.zeros((1, HG_W), F32)
    for j in range(1, layer + 1):
        lb = lb + e[j:j + 1] / den

    fr = f_ref[...]
    f = lb + (1.0 - lb) * _sigmoid(fr)
    logf2 = jnp.log2(f)
    kk = (1.0 - lb) * _sigmoid(-fr)
    if t_valid < tb:
        live = lax.broadcasted_iota(jnp.int32, (tb, HG_W), 0) < t_valid
        logf2 = jnp.where(live, logf2, 0.0)
        kk = jnp.where(live, kk, 0.0)
    kk_ref[...] = kk

    grp = min(LANES, tb)
    ri = lax.broadcasted_iota(jnp.int32, (grp, grp), 0)
    ci = lax.broadcasted_iota(jnp.int32, (grp, grp), 1)
    tri = jnp.where((ri // c == ci // c) & (ci <= ri), 1.0, 0.0).astype(BF16)
    for r in range(tb // grp):
        rem = logf2[r * grp:(r + 1) * grp]
        acc = None
        for _ in range(3):
            piece = rem.astype(BF16)
            rem = rem - piece.astype(F32)
            y = jnp.dot(tri, piece, preferred_element_type=F32)
            acc = y if acc is None else acc + y
        lc_ref[r * grp:(r + 1) * grp, :] = acc

    bi = lax.broadcasted_iota(jnp.int32, (half, half), 0) // HG_DK
    bj = lax.broadcasted_iota(jnp.int32, (half, half), 1) // HG_DK
    bd = jnp.where(bi == bj, 1.0, 0.0)
    bd16 = bd.astype(BF16)
    head_lanes = lax.broadcasted_iota(jnp.int32, (c, half), 1) // HG_DK

    def chunk_step(cidx):
        r0 = pl.multiple_of(cidx * c, c)
        qc = q_ref[pl.ds(r0, c), :]
        vc = i_ref[pl.ds(r0, c), :]
        kc = kk_ref[pl.ds(r0, c), :]
        lc = lc_ref[pl.ds(r0, c), :]
        last = lc[c - 1:c, :]
        qt = (qc * jnp.exp2(lc)).astype(BF16)
        o = jnp.concatenate(
            [lax.dot_general(qt[:, g * half:(g + 1) * half], st_ref[g].astype(BF16), CONTRACT_LAST,
                             preferred_element_type=F32) for g in range(2)], axis=1)
        ws = []
        for s in range(c):
            t0 = (s // SUBLANES) * SUBLANES
            dec = jnp.exp2(lc[t0:] - lc[s:s + 1, :])
            if t0:
                ws.append(jnp.zeros((t0, HG_W), F32))
            live = lax.broadcasted_iota(jnp.int32, (c - t0, HG_W), 0) >= s - t0
            ws.append(jnp.where(live, qc[t0:] * kc[s:s + 1, :] * dec, 0.0))
        w = jnp.concatenate(ws, axis=0).astype(BF16)
        att = jnp.concatenate(
            [jnp.dot(w[:, g * half:(g + 1) * half], bd16, preferred_element_type=F32) for g in range(2)], axis=1)
        tiles = [o[t0:t0 + SUBLANES] for t0 in range(0, c, SUBLANES)]
        for s in range(c):
            for ti in range(s // SUBLANES, c // SUBLANES):
                off = s * c + ti * SUBLANES
                tiles[ti] = tiles[ti] + att[off:off + SUBLANES] * vc[s:s + 1, :]
        o_ref[pl.ds(r0, c), :] = jnp.concatenate(tiles, axis=0)
        k2 = (kc * jnp.exp2(last - lc)).astype(BF16)
        vb = vc.astype(BF16)
        dl = jnp.exp2(last)
        for g in range(2):
            vg = vb[:, g * half:(g + 1) * half]
            kg = k2[:, g * half:(g + 1) * half]
            v_exp = jnp.concatenate([jnp.where(head_lanes == hh, vg, 0.0) for hh in range(4)], axis=0)
            k_exp = jnp.concatenate([jnp.where(head_lanes == hh, kg, 0.0) for hh in range(4)], axis=0)
            upd = lax.dot_general(v_exp, k_exp, CONTRACT_FIRST, preferred_element_type=F32)
            st_ref[g] = st_ref[g] * dl[:, g * half:(g + 1) * half] + upd

    def finish():
        o = o_ref[...]
        o2 = o * o
        ms = jnp.concatenate([_split_dot(o2[:, g * half:(g + 1) * half], bd, 3) for g in range(2)],
                             axis=1) * (1.0 / HG_DK)
        og = g_ref[...]
        o_ref[...] = o * lax.rsqrt(ms + RMS_EPS) * gain_ref[...] * (og * _sigmoid(og))

        @pl.when(t == n_t - 1)
        def _():
            so_ref[...] = st_ref[...]

    return chunk_step, finish


def _gla_call(h3, lb_logits, gain, s0_bd, layer, tb, c, t_valid, bb):
    b, s, _ = h3.shape
    n_t = s // tb
    n_layers = lb_logits.shape[0]
    assert b % bb == 0
    col = lambda cidx: pl.BlockSpec((bb, tb, COLW), lambda bi, ti, cidx=cidx: (bi, ti, cidx))
    full2 = lambda shape: pl.BlockSpec(shape, lambda bi, ti: (0, 0))
    st_spec = pl.BlockSpec((bb, 2, MXU_DIM, MXU_DIM), lambda bi, ti: (bi, 0, 0, 0))
    return pl.pallas_call(
        functools.partial(_gla_kernel, bb=bb, layer=layer, tb=tb, c=c, t_valid=t_valid, n_t=n_t),
        grid=(b // bb, n_t),
        in_specs=[col(C_QH), col(C_FH), col(C_IH), col(C_GH),
                  full2((n_layers, HG_W)), full2((1, HG_W)), st_spec],
        out_specs=[pl.BlockSpec((bb, tb, HG_W), lambda bi, ti: (bi, ti, 0)), st_spec],
        out_shape=[jax.ShapeDtypeStruct((b, s, HG_W), F32),
                   jax.ShapeDtypeStruct((b, 2, MXU_DIM, MXU_DIM), F32)],
        scratch_shapes=[pltpu.VMEM((bb, 2, MXU_DIM, MXU_DIM), F32),
                        pltpu.VMEM((bb, tb, HG_W), F32),
                        pltpu.VMEM((bb, tb, HG_W), F32)],
        compiler_params=_params(2),
        name="hgrn2",
    )(h3, h3, h3, h3, lb_logits, gain, s0_bd)


def _state_to_bd(s):
    b = s.shape[0]
    st = jnp.swapaxes(s, -1, -2).reshape(b, 2, 4, HG_DK, 1, HG_DK)
    eye = jnp.eye(4, dtype=F32).reshape(1, 1, 4, 1, 4, 1)
    return (st * eye).reshape(b, 2, MXU_DIM, MXU_DIM)


def _state_from_bd(st):
    b = st.shape[0]
    st6 = st.reshape(b, 2, 4, HG_DK, 4, HG_DK)
    blocks = jnp.stack([st6[:, :, hh, :, hh, :] for hh in range(4)], axis=2)
    return jnp.swapaxes(blocks.reshape(b, HG_HEADS, HG_DK, HG_DK), -1, -2)


def _lru_kernel(x_ref, g_ref, cw_ref, cb_ref, wa_ref, ba_ref, wx_ref, bx_ref, lam_ref, h0_ref, buf0_ref,
                y_ref, hl_ref, bo_ref, xp_ref, a_ref, u_ref, hc_ref, *, tb, t_valid, pad, n_t):
    t = pl.program_id(1)
    nprev = CONV_W - 1
    base = SUBLANES

    @pl.when(t == 0)
    def _():
        xp_ref[base - nprev:base, :] = buf0_ref[...]
        hc_ref[...] = h0_ref[...]

    x = x_ref[...]
    xp_ref[base:base + tb, :] = x
    cw = cw_ref[...]
    xc = cb_ref[...] + cw[nprev:nprev + 1, :] * x
    for j in range(nprev):
        xc = xc + cw[j:j + 1, :] * xp_ref[base - nprev + j:base - nprev + j + tb, :]
    bo_ref[...] = xp_ref[base + t_valid - nprev:base + t_valid, :]
    xp_ref[base - nprev:base, :] = xp_ref[base + tb - nprev:base + tb, :]

    xb = xc.astype(BF16)
    r = _sigmoid(jnp.dot(xb, wa_ref[...], preferred_element_type=F32) + ba_ref[...])
    ig = _sigmoid(jnp.dot(xb, wx_ref[...], preferred_element_type=F32) + bx_ref[...])
    z = -lam_ref[...]
    softplus = jnp.maximum(z, 0.0) + jnp.log(1.0 + jnp.exp(-jnp.abs(z)))
    log_a = -LRU_C * r * softplus
    a = jnp.exp(log_a)
    u = jnp.sqrt(jnp.maximum(1.0 - jnp.exp(2.0 * log_a), SQRT_EPS)) * ig * xc
    first = lax.broadcasted_iota(jnp.int32, (tb, LRU_W), 0) == 0
    u = u + jnp.where(first, a * hc_ref[...], 0.0)

    a_ref[0:pad, :] = jnp.ones((pad, LRU_W), F32)
    u_ref[0:pad, :] = jnp.zeros((pad, LRU_W), F32)
    a_ref[pad:pad + tb, :] = a
    u_ref[pad:pad + tb, :] = u
    d = 1
    while d < tb:
        a_cur = a_ref[pad:pad + tb, :]
        u_cur = u_ref[pad:pad + tb, :]
        a_sh = a_ref[pad - d:pad - d + tb, :]
        u_sh = u_ref[pad - d:pad - d + tb, :]
        u_ref[pad:pad + tb, :] = a_cur * u_sh + u_cur
        a_ref[pad:pad + tb, :] = a_cur * a_sh
        d *= 2
    h = u_ref[pad:pad + tb, :]
    y_ref[...] = h * _gelu(g_ref[...])
    hc_ref[...] = h[tb - 1:tb, :]
    hl_ref[...] = h[t_valid - 1:t_valid, :]


def _lru_call(h3, w, h0, buf0, tb, t_valid):
    b, s, _ = h3.shape
    n_t = s // tb
    pad = max(SUBLANES, tb // 2)
    cw, cb, wa_bd, ba, wx_bd, bx, lam = w
    col = lambda cidx: pl.BlockSpec((None, tb, COLW), lambda bi, ti, cidx=cidx: (bi, ti, cidx))
    full2 = lambda shape: pl.BlockSpec(shape, lambda bi, ti: (0, 0))
    vec = full2((1, LRU_W))
    per_b = lambda r: pl.BlockSpec((None, r, LRU_W), lambda bi, ti: (bi, 0, 0))
    return pl.pallas_call(
        functools.partial(_lru_kernel, tb=tb, t_valid=t_valid, pad=pad, n_t=n_t),
        grid=(b, n_t),
        in_specs=[col(C_XL), col(C_GL), full2((CONV_W, LRU_W)), vec, full2((LRU_W, LRU_W)), vec,
                  full2((LRU_W, LRU_W)), vec, vec, per_b(1), per_b(CONV_W - 1)],
        out_specs=[pl.BlockSpec((None, tb, LRU_W), lambda bi, ti: (bi, ti, 0)), per_b(1), per_b(CONV_W - 1)],
        out_shape=[jax.ShapeDtypeStruct((b, s, LRU_W), F32),
                   jax.ShapeDtypeStruct((b, 1, LRU_W), F32),
                   jax.ShapeDtypeStruct((b, CONV_W - 1, LRU_W), F32)],
        scratch_shapes=[pltpu.VMEM((SUBLANES + tb, LRU_W), F32),
                        pltpu.VMEM((pad + tb, LRU_W), F32),
                        pltpu.VMEM((pad + tb, LRU_W), F32),
                        pltpu.VMEM((1, LRU_W), F32)],
        compiler_params=_params(2),
        name="rglru",
    )(h3, h3, cw, cb, wa_bd, ba, wx_bd, bx, lam, h0, buf0)


def _block_diag(w):
    eye = jnp.eye(LRU_BLOCKS, dtype=w.dtype).reshape(LRU_BLOCKS, 1, LRU_BLOCKS, 1)
    return (w[:, :, None, :] * eye).reshape(LRU_W, LRU_W)


def _merge_kernel(x_ref, oa_ref, oh_ref, ol_ref, g0a, g0b, g1a, g1b, g2a, g2b,
                  wa_ref, wh_ref, wl_ref, wo_ref, lg_ref, lb_ref, y_ref, *, alpha):
    def branch(o_ref, w_ref, ga, gb):
        y = jnp.dot(o_ref[...].astype(BF16), w_ref[...], preferred_element_type=F32)
        gate = jnp.concatenate([ga[...], gb[...]], axis=1)
        return _sigmoid(gate) * y

    merged = branch(oa_ref, wa_ref, g0a, g0b) + branch(oh_ref, wh_ref, g1a, g1b) + branch(ol_ref, wl_ref, g2a, g2b)
    mix = jnp.dot(merged.astype(BF16), wo_ref[...], preferred_element_type=F32)
    y_ref[...] = _layer_norm(alpha * x_ref[...] + mix, lg_ref[...], lb_ref[...])


def _merge_call(x2, h2, o_att, o_hg, o_lru, w, tm, alpha):
    m = x2.shape[0]
    w_att, w_hg, w_lru, w_o, ln_g, ln_b = w
    row = lambda width: pl.BlockSpec((tm, width), lambda i: (i, 0))
    col = lambda cidx: pl.BlockSpec((tm, COLW), lambda i, cidx=cidx: (i, cidx))
    full = lambda shape: pl.BlockSpec(shape, lambda i: (0, 0))
    return pl.pallas_call(
        functools.partial(_merge_kernel, alpha=alpha),
        grid=(m // tm,),
        in_specs=[row(D_MODEL), row(ATT_W), row(HG_W), row(LRU_W)]
                 + [col(C_GM + j) for j in range(6)]
                 + [full((ATT_W, D_MODEL)), full((HG_W, D_MODEL)), full((LRU_W, D_MODEL)),
                    full((D_MODEL, D_MODEL)), full((1, D_MODEL)), full((1, D_MODEL))],
        out_specs=row(D_MODEL),
        out_shape=jax.ShapeDtypeStruct((m, D_MODEL), F32),
        compiler_params=_params(1),
        name="merge_ln",
    )(x2, o_att, o_hg, o_lru, h2, h2, h2, h2, h2, h2, w_att, w_hg, w_lru, w_o, ln_g, ln_b)


def _ffn_kernel(x_ref, wup_ref, cw_ref, cb_ref, wdn_ref, lg_ref, lb_ref, p2_ref, p1_ref,
                y_ref, tail_ref, carry_ref, *, tm, ts, fc, alpha, per_row_prev, n_t):
    i = pl.program_id(0)
    x = x_ref[...]
    xb = x.astype(BF16)
    tpos = lax.broadcasted_iota(jnp.int32, (tm, fc), 0) % ts
    cw = cw_ref[...]
    cb = cb_ref[...]

    if not per_row_prev:
        @pl.when(i % n_t == 0)
        def _():
            carry_ref[...] = p2_ref[...]

    acc = jnp.zeros((tm, D_MODEL), F32)
    for cidx in range(D_FF // fc):
        lo, hi = cidx * fc, (cidx + 1) * fc
        u = jnp.dot(xb, wup_ref[:, lo:hi], preferred_element_type=F32)
        val = jnp.dot(xb, wup_ref[:, D_FF + lo:D_FF + hi], preferred_element_type=F32)
        if per_row_prev:
            prev2 = p2_ref[:, lo:hi]
            prev1 = p1_ref[:, lo:hi]
        else:
            cm2 = carry_ref[0:1, lo:hi]
            cm1 = carry_ref[1:2, lo:hi]
            prev2 = jnp.where(tpos == 0, cm2, cm1)
            prev1 = jnp.broadcast_to(cm1, (tm, fc))
        u1 = jnp.where(tpos >= 1, pltpu.roll(u, 1, 0), prev1)
        u2 = jnp.where(tpos >= 2, pltpu.roll(u, 2, 0), prev2)
        uc = cb[:, lo:hi] + cw[0:1, lo:hi] * u2 + cw[1:2, lo:hi] * u1 + cw[2:3, lo:hi] * u
        if per_row_prev:
            tail_ref[:, lo:hi] = u
        else:
            carry_ref[:, lo:hi] = u[tm - 2:tm, :]
            tail_ref[:, lo:hi] = u[tm - SUBLANES:tm, :]
        gated = (_gelu(uc) * val).astype(BF16)
        acc = acc + jnp.dot(gated, wdn_ref[lo:hi, :], preferred_element_type=F32)
    y_ref[...] = _layer_norm(alpha * x + acc, lg_ref[...], lb_ref[...])


def _ffn_call(x2, w, prev2, prev1, tm, ts, n_seq, alpha, per_row_prev):
    m = x2.shape[0]
    w_up, cw, cb, w_dn, ln_g, ln_b = w
    n_t = (m // n_seq) // tm if not per_row_prev else 1
    fc = 1024
    row = pl.BlockSpec((tm, D_MODEL), lambda i: (i, 0))
    full = lambda shape: pl.BlockSpec(shape, lambda i: (0, 0))
    resident = lambda shape: pl.BlockSpec(shape, lambda i: (0, 0), pipeline_mode=pl.Buffered(1))
    if per_row_prev:
        p2_spec = pl.BlockSpec((tm, D_FF), lambda i: (i, 0))
        p1_spec = pl.BlockSpec((tm, D_FF), lambda i: (i, 0))
        tail_spec = pl.BlockSpec((tm, D_FF), lambda i: (i, 0))
        tail_shape = jax.ShapeDtypeStruct((m, D_FF), F32)
    else:
        p2_spec = pl.BlockSpec((None, FFN_CONV_W - 1, D_FF), lambda i: (i // n_t, 0, 0))
        p1_spec = pl.BlockSpec((None, FFN_CONV_W - 1, D_FF), lambda i: (i // n_t, 0, 0))
        tail_spec = pl.BlockSpec((None, SUBLANES, D_FF), lambda i: (i // n_t, 0, 0))
        tail_shape = jax.ShapeDtypeStruct((n_seq, SUBLANES, D_FF), F32)
    return pl.pallas_call(
        functools.partial(_ffn_kernel, tm=tm, ts=ts, fc=fc, alpha=alpha, per_row_prev=per_row_prev, n_t=n_t),
        grid=(m // tm,),
        in_specs=[row, resident((D_MODEL, 2 * D_FF)), full((FFN_CONV_W, D_FF)), full((1, D_FF)),
                  resident((D_FF, D_MODEL)), full((1, D_MODEL)), full((1, D_MODEL)), p2_spec, p1_spec],
        out_specs=[row, tail_spec],
        out_shape=[jax.ShapeDtypeStruct((m, D_MODEL), F32), tail_shape],
        scratch_shapes=[pltpu.VMEM((FFN_CONV_W - 1, D_FF), F32)],
        compiler_params=_params(1),
        name="ffn_ln",
    )(x2, w_up, cw, cb, w_dn, ln_g, ln_b, prev2, prev1)


def _trunk_layer(x3, tabs, attn_fn, hg_s0, lru_h0, lru_buf0, ffn_buf0, wl, layer, cfg):
    b, s, _ = x3.shape
    m = b * s
    x2 = x3.reshape(m, D_MODEL)
    h2 = _matmul(x2, wl["w_in"], cfg["bm"], cfg["bn"])
    h3 = h2.reshape(b, s, D_IN)
    rope_out = _rope_call(h3, tabs, cfg["rope_tm"], cfg["prompt"])
    k_rot, v_f32 = rope_out[1], rope_out[2]
    o_att = attn_fn(*rope_out)
    o_hg, hg_st = _gla_call(h3, wl["hg_lb_logits"], wl["hg_gain"], hg_s0, layer,
                            cfg["seq_tb"], cfg["gla_c"], cfg["t_valid"], cfg["gla_bb"])
    o_lru, lru_h, lru_buf = _lru_call(h3, wl["lru"], lru_h0, lru_buf0, cfg["seq_tb"], cfg["t_valid"])
    x1 = _merge_call(x2, h2, o_att.reshape(m, ATT_W), o_hg.reshape(m, HG_W), o_lru.reshape(m, LRU_W),
                     wl["merge"], cfg["tok_tm"], cfg["alpha"])
    if cfg["prompt"]:
        x_out, tail = _ffn_call(x1, wl["ffn"], ffn_buf0, ffn_buf0, cfg["tok_tm"], cfg["tok_tm"], b,
                                cfg["alpha"], False)
        ffn_buf = tail[:, SUBLANES - (FFN_CONV_W - 1):, :]
    else:
        tv = cfg["t_valid"]
        zrow = lambda n: jnp.zeros((b, n, D_FF), F32)
        prev2 = jnp.concatenate([ffn_buf0, zrow(s - 2)], axis=1).reshape(m, D_FF)
        prev1 = jnp.concatenate([ffn_buf0[:, 1:2], zrow(s - 1)], axis=1).reshape(m, D_FF)
        x_out, u_all = _ffn_call(x1, wl["ffn"], prev2, prev1, m, s, b, cfg["alpha"], True)
        ffn_buf = u_all.reshape(b, s, D_FF)[:, tv - (FFN_CONV_W - 1):tv]
    return (x_out.reshape(b, s, D_MODEL), k_rot, v_f32, hg_st, lru_h.reshape(b, LRU_W), lru_buf, ffn_buf)


def kernel(x_prompt, x_sample, cache_k, cache_v, page_table, state_hgrn, state_lru_h, state_lru_conv, state_ffn_conv, w_in, hg_lb_logits, hg_gain, lru_conv_w, lru_conv_b, lru_wa, lru_ba, lru_wx, lru_bx, lru_lambda, w_br_att, w_br_hg, w_br_lru, w_o, ln1_g, ln1_b, ffn_w_up, ffn_conv_w, ffn_conv_b, ffn_w_down, ln2_g, ln2_b):
    bsz, seq, _ = x_prompt.shape
    dbs, n_tok, _ = x_sample.shape
    depth = w_in.shape[0]
    page = cache_k.shape[2]
    past = page_table.shape[1] * page
    assert seq % MOBA_BLOCK == 0 and past % MOBA_BLOCK == 0
    assert CONV_W - 1 <= n_tok <= SUBLANES
    alpha = (2.0 * depth) ** 0.25
    tpad = SUBLANES
    n_blk = seq // MOBA_BLOCK
    nbp = -(-n_blk // SUBLANES) * SUBLANES

    cache_kt = jnp.transpose(cache_k, (0, 1, 3, 4, 2))
    cache_vt = jnp.transpose(cache_v, (0, 1, 3, 4, 2))
    tabs_p = _rope_tables(jnp.arange(seq))
    tabs_s = _rope_tables(past + jnp.arange(tpad))

    seq_tb = min(512, seq)
    assert seq % seq_tb == 0 and (bsz * seq) % min(1024, bsz * seq) == 0
    cfg_p = dict(prompt=True, bm=min(1024, bsz * seq), bn=1536, rope_tm=min(512, seq), seq_tb=seq_tb,
                 gla_c=16, gla_bb=math.gcd(bsz, 2), t_valid=seq_tb, tok_tm=min(512, seq), alpha=alpha)
    cfg_s = dict(prompt=False, bm=dbs * tpad, bn=1536, rope_tm=tpad, seq_tb=tpad,
                 gla_c=tpad, gla_bb=math.gcd(dbs, 4), t_valid=n_tok, tok_tm=dbs * tpad, alpha=alpha)

    xp = x_prompt
    xs = jnp.pad(x_sample, ((0, 0), (0, tpad - n_tok), (0, 0)))
    hg0_p = jnp.zeros((bsz, 2, MXU_DIM, MXU_DIM), F32)
    lh0_p = jnp.zeros((bsz, 1, LRU_W), F32)
    lb0_p = jnp.zeros((bsz, CONV_W - 1, LRU_W), F32)
    fb0_p = jnp.zeros((bsz, FFN_CONV_W - 1, D_FF), F32)

    def attn_p(q_rot, k_rot, v_f32, k_bf, v_t, ksum):
        ks = jnp.pad(ksum.reshape(bsz, n_blk, ATT_W), ((0, 0), (0, nbp - n_blk), (0, 0)))
        return _attn_prompt_call(q_rot, k_bf, v_t, ks)

    outs_p = [[] for _ in range(6)]
    outs_s = [[] for _ in range(6)]
    row = lambda a: a.reshape(1, -1)
    for l in range(depth):
        wl = dict(
            w_in=w_in[l].astype(BF16),
            hg_lb_logits=hg_lb_logits,
            hg_gain=row(hg_gain[l]),
            lru=(lru_conv_w[l], row(lru_conv_b[l]), _block_diag(lru_wa[l]).astype(BF16), row(lru_ba[l]),
                 _block_diag(lru_wx[l]).astype(BF16), row(lru_bx[l]), row(lru_lambda[l])),
            merge=(w_br_att[l].astype(BF16), w_br_hg[l].astype(BF16), w_br_lru[l].astype(BF16),
                   w_o[l].astype(BF16), row(ln1_g[l]), row(ln1_b[l])),
            ffn=(ffn_w_up[l].astype(BF16), ffn_conv_w[l], row(ffn_conv_b[l]), ffn_w_down[l].astype(BF16),
                 row(ln2_g[l]), row(ln2_b[l])),
        )

        def attn_s(q_rot, k_rot, v_f32, l=l):
            return _attn_sample_call(page_table, q_rot, k_rot, v_f32, cache_kt, cache_vt, l, n_tok)

        xp, *new_p = _trunk_layer(xp, tabs_p, attn_p, hg0_p, lh0_p, lb0_p, fb0_p, wl, l, cfg_p)
        xs, *new_s = _trunk_layer(xs, tabs_s, attn_s, _state_to_bd(state_hgrn[:, l]),
                                  state_lru_h[:, l].reshape(dbs, 1, LRU_W), state_lru_conv[:, l],
                                  state_ffn_conv[:, l], wl, l, cfg_s)
        for lst, a in zip(outs_p, new_p):
            lst.append(a)
        for lst, a in zip(outs_s, new_s):
            lst.append(a)

    def heads(a, t):
        return a.reshape(a.shape[0], a.shape[1], t, ATT_HEADS, HEAD_DIM)

    def heads_t(a):
        return jnp.transpose(a.reshape(bsz, depth, ATT_HEADS, HEAD_DIM, seq), (0, 1, 4, 2, 3))

    k_p = heads_t(jnp.stack(outs_p[0], axis=1))
    v_p = heads_t(jnp.stack(outs_p[1], axis=1))
    k_s = heads(jnp.stack([a[:, :n_tok] for a in outs_s[0]], axis=1), n_tok)
    v_s = heads(jnp.stack([a[:, :n_tok] for a in outs_s[1]], axis=1), n_tok)
    hg_p = jnp.stack([_state_from_bd(a) for a in outs_p[2]], axis=1)
    hg_s = jnp.stack([_state_from_bd(a) for a in outs_s[2]], axis=1)
    lh_p = jnp.stack(outs_p[3], axis=1)
    lh_s = jnp.stack(outs_s[3], axis=1)
    lc_p = jnp.stack(outs_p[4], axis=1)
    lc_s = jnp.stack(outs_s[4], axis=1)
    fc_p = jnp.stack(outs_p[5], axis=1)
    fc_s = jnp.stack(outs_s[5], axis=1)
    return (xp, xs[:, :n_tok], k_p, v_p, k_s, v_s, hg_p, hg_s, lh_p, lh_s, lc_p, lc_s, fc_p, fc_s)
```

```python
import functools
import math

import jax
import jax.numpy as jnp
from jax import lax
from jax.experimental import pallas as pl
from jax.experimental.pallas import tpu as pltpu

F32 = jnp.float32
BF16 = jnp.bfloat16

D_MODEL = 1024
ATT_HEADS = 8
HEAD_DIM = 64
ATT_W = ATT_HEADS * HEAD_DIM
MOBA_BLOCK = 256
MOBA_TOPK = 3
ROPE_THETA = 500000.0
ROPE_DIM = HEAD_DIM // 4
HG_HEADS = 8
HG_DK = 64
HG_W = HG_HEADS * HG_DK
LRU_W = 512
LRU_BLOCKS = 8
LRU_BD = LRU_W // LRU_BLOCKS
LRU_C = 8.0
CONV_W = 4
D_FF = 3 * D_MODEL
FFN_CONV_W = 3
N_BRANCH = 3
LN_EPS = 1e-5
RMS_EPS = 1e-6
SQRT_EPS = 1e-12
NEG_INF = -1e30
D_IN = 3 * ATT_W + 4 * HG_W + 2 * LRU_W + N_BRANCH * D_MODEL
LOG2E = math.log2(math.e)
ATTN_GROUP = 2
SAMPLE_BLOCKS_PER_STEP = 4

COLW = 512
C_QA, C_KA, C_VA, C_QH, C_FH, C_IH, C_GH, C_XL, C_GL, C_GM = 0, 1, 2, 3, 4, 5, 6, 7, 8, 9

LANES = 128
SUBLANES = 8
MXU_DIM = 256
VMEM_LIMIT = 56 * 1024 * 1024

HIGHEST = lax.Precision.HIGHEST
CONTRACT_LAST = (((1,), (1,)), ((), ()))
CONTRACT_FIRST = (((0,), (0,)), ((), ()))


def _params(n_grid):
    return pltpu.CompilerParams(dimension_semantics=("arbitrary",) * n_grid,
                                vmem_limit_bytes=VMEM_LIMIT)


def _sigmoid(x):
    return 1.0 / (1.0 + jnp.exp(-x))


def _gelu(x):
    c = math.sqrt(2.0 / math.pi)
    return x * (0.5 * (1.0 + jnp.tanh(c * (x + 0.044715 * (x * x * x)))))


def _layer_norm(y, g, b):
    mu = jnp.mean(y, axis=-1, keepdims=True)
    d = y - mu
    var = jnp.mean(d * d, axis=-1, keepdims=True)
    return d * lax.rsqrt(var + LN_EPS) * g + b


def _split_dot(x, w01, parts):
    w = w01.astype(BF16)
    acc = None
    rem = x
    for _ in range(parts):
        piece = rem.astype(BF16)
        rem = rem - piece.astype(F32)
        y = jnp.dot(piece, w, preferred_element_type=F32)
        acc = y if acc is None else acc + y
    return acc


def _top_k_mask(gate, valid, axis):
    idx_all = lax.broadcasted_iota(jnp.int32, gate.shape, axis).astype(F32)
    g = jnp.where(valid, gate, -jnp.inf)
    sel = jnp.zeros(gate.shape, F32)
    for _ in range(MOBA_TOPK):
        mx = jnp.max(g, axis=axis, keepdims=True)
        idx = jnp.min(jnp.where(g == mx, idx_all, float(gate.shape[axis])), axis=axis, keepdims=True)
        pick = idx_all == idx
        sel = jnp.where(pick, 1.0, sel)
        g = jnp.where(pick, -jnp.inf, g)
    return jnp.where(valid, sel, 0.0)


def _mm_kernel(x_ref, w_ref, o_ref):
    o_ref[...] = jnp.dot(x_ref[...].astype(BF16), w_ref[...], preferred_element_type=F32)


def _matmul(x, w, layer, bm, bn):
    m, k = x.shape
    n = w.shape[2]
    assert m % bm == 0 and n % bn == 0
    return pl.pallas_call(
        _mm_kernel,
        grid=(m // bm, n // bn),
        in_specs=[pl.BlockSpec((bm, k), lambda i, j: (i, 0)),
                  pl.BlockSpec((None, k, bn), lambda i, j: (layer, 0, j))],
        out_specs=pl.BlockSpec((bm, bn), lambda i, j: (i, j)),
        out_shape=jax.ShapeDtypeStruct((m, n), F32),
        compiler_params=_params(2),
        name="in_proj",
    )(x, w)


def _rope_kernel(q_ref, k_ref, v_ref, c_ref, sa_ref, sb_ref, qo_ref, ko_ref, vo_ref, *extra, n_blk):
    c = c_ref[...]
    sa = sa_ref[...]
    sb = sb_ref[...]

    def rot(x):
        outs = []
        for j in range(ATT_W // LANES):
            xs = x[:, LANES * j:LANES * (j + 1)]
            outs.append(xs * c + pltpu.roll(xs, ROPE_DIM // 2, 1) * sa
                        + pltpu.roll(xs, LANES - ROPE_DIM // 2, 1) * sb)
        return jnp.concatenate(outs, axis=1)

    q = rot(q_ref[...])
    k = rot(k_ref[...])
    v = v_ref[...]
    qo_ref[...] = q
    if n_blk:
        kb_ref, vt_ref, ks_ref = extra
        kb_ref[...] = k.astype(BF16)
        v_t = v.T
        ko_ref[...] = k.T
        vo_ref[...] = v_t
        for r in range(n_blk):
            rows = slice(MOBA_BLOCK * r, MOBA_BLOCK * (r + 1))
            ks_ref[r] = jnp.sum(k[rows], axis=0, keepdims=True)
            vt_ref[r] = v_t[:, rows].astype(BF16)
    else:
        ko_ref[...] = k
        vo_ref[...] = v


def _rope_call(h3, tabs, tm, prompt):
    b, s, _ = h3.shape
    nt = s // tm
    n_blk = tm // MOBA_BLOCK if prompt else 0
    col = lambda cidx: pl.BlockSpec((None, tm, COLW), lambda bi, ti, cidx=cidx: (bi, ti, cidx))
    tab = pl.BlockSpec((tm, LANES), lambda bi, ti: (ti, 0))
    row = pl.BlockSpec((None, tm, ATT_W), lambda bi, ti: (bi, ti, 0))
    out_specs = [row, row, row]
    out_shape = [jax.ShapeDtypeStruct((b, s, ATT_W), F32)] * 3
    if prompt:
        row_t = pl.BlockSpec((None, ATT_W, tm), lambda bi, ti: (bi, 0, ti))
        out_specs = [row, row_t, row_t]
        out_shape = [jax.ShapeDtypeStruct((b, s, ATT_W), F32)] + [jax.ShapeDtypeStruct((b, ATT_W, s), F32)] * 2
        out_specs += [row,
                      pl.BlockSpec((None, n_blk, ATT_W, MOBA_BLOCK), lambda bi, ti: (bi, ti, 0, 0)),
                      pl.BlockSpec((None, n_blk, 1, ATT_W), lambda bi, ti: (bi, ti, 0, 0))]
        out_shape += [jax.ShapeDtypeStruct((b, s, ATT_W), BF16),
                      jax.ShapeDtypeStruct((b, s // MOBA_BLOCK, ATT_W, MOBA_BLOCK), BF16),
                      jax.ShapeDtypeStruct((b, s // MOBA_BLOCK, 1, ATT_W), F32)]
    return pl.pallas_call(
        functools.partial(_rope_kernel, n_blk=n_blk),
        grid=(b, nt),
        in_specs=[col(C_QA), col(C_KA), col(C_VA), tab, tab, tab],
        out_specs=out_specs,
        out_shape=out_shape,
        compiler_params=_params(2),
        name="rope",
    )(h3, h3, h3, *tabs)


def _rope_tables(pos):
    half = ROPE_DIM // 2
    inv = ROPE_THETA ** (-jnp.arange(half, dtype=F32) / half)
    ang = pos.astype(F32)[:, None] * inv[None, :]
    cos, sin = jnp.cos(ang), jnp.sin(ang)
    lh = jnp.arange(LANES) % HEAD_DIM
    fi = lh % half
    c = jnp.where(lh[None, :] < ROPE_DIM, cos[:, fi], 1.0)
    sa = jnp.where((lh[None, :] >= half) & (lh[None, :] < ROPE_DIM), sin[:, fi], 0.0)
    sb = jnp.where(lh[None, :] < half, -sin[:, fi], 0.0)
    return c.astype(F32), sa.astype(F32), sb.astype(F32)


def _attn_prompt_kernel(q_ref, k_ref, vt_ref, ks_ref, o_ref, bias_ref, sa_ref, sb_ref, *, tq, nbp, nb, n_blk):
    i = pl.program_id(2)
    qscale = HEAD_DIM ** -0.5 * LOG2E
    q = q_ref[...]
    kmean = ks_ref[...] * (1.0 / MOBA_BLOCK)
    lane = lax.broadcasted_iota(jnp.int32, (tq, LANES), 1)
    valid = lax.broadcasted_iota(jnp.int32, (nbp, tq), 0) < i
    causal = (lax.broadcasted_iota(jnp.int32, (tq, tq), 0) <= lax.broadcasted_iota(jnp.int32, (tq, tq), 1))
    own0 = pl.multiple_of(i * tq, tq)
    k_own = k_ref[pl.ds(own0, tq), :]
    vt_own = vt_ref[i]

    qs = []
    state = []
    for hh in range(2):
        in_head = (lane >= hh * HEAD_DIM) & (lane < (hh + 1) * HEAD_DIM)
        qh = jnp.where(in_head, q, 0.0)
        gate_t = lax.dot_general(kmean, qh, CONTRACT_LAST, precision=HIGHEST, preferred_element_type=F32)
        sel = _top_k_mask(gate_t, valid, 0)
        bias_ref[hh, 0:nbp, :] = jnp.where(sel > 0.0, 0.0, NEG_INF)
        qsh = (qh * qscale).astype(BF16)
        qs.append(qsh)
        s = lax.dot_general(k_own, qsh, CONTRACT_LAST, preferred_element_type=F32)
        s = jnp.where(causal, s, NEG_INF)
        m = jnp.max(s, axis=0, keepdims=True)
        p = jnp.exp2(s - m)
        l = jnp.sum(p, axis=0, keepdims=True)
        acc = jnp.dot(vt_own[hh * HEAD_DIM:(hh + 1) * HEAD_DIM, :], p.astype(BF16), preferred_element_type=F32)
        state += [m, l, acc]

    neg_rows = jnp.full((SUBLANES, tq), NEG_INF, F32)
    bias_ref[0, nbp:nbp + SUBLANES, :] = neg_rows
    bias_ref[1, nbp:nbp + SUBLANES, :] = neg_rows

    def form_scores(jg, s_ref):
        gmax = [None, None]
        for r in range(nb):
            j = jg * nb + r
            jk = jnp.minimum(j, n_blk - 1)
            jb = jnp.minimum(j, nbp)
            kj = k_ref[pl.ds(pl.multiple_of(jk * tq, tq), tq), :]
            for hh in range(2):
                s = (lax.dot_general(kj, qs[hh], CONTRACT_LAST, preferred_element_type=F32)
                     + bias_ref[hh, pl.ds(jb, 1), :])
                s_ref[r * 2 + hh] = s
                mx = jnp.max(s, axis=0, keepdims=True)
                gmax[hh] = mx if gmax[hh] is None else jnp.maximum(gmax[hh], mx)
        return gmax

    def softmax_pv(jg, s_ref, gmax, st):
        out = []
        for hh in range(2):
            m, l, acc = st[3 * hh:3 * hh + 3]
            m_new = jnp.maximum(m, gmax[hh])
            alpha = jnp.exp2(m - m_new)
            l = alpha * l
            acc = alpha * acc
            for r in range(nb):
                vtj = vt_ref[jnp.minimum(jg * nb + r, n_blk - 1)]
                p = jnp.exp2(s_ref[r * 2 + hh] - m_new)
                l = l + jnp.sum(p, axis=0, keepdims=True)
                acc = acc + jnp.dot(vtj[hh * HEAD_DIM:(hh + 1) * HEAD_DIM, :], p.astype(BF16),
                                    preferred_element_type=F32)
            out += [m_new, l, acc]
        return out

    def body(it, carry):
        st, gmax_a = list(carry[:6]), list(carry[6:])
        gmax_b = form_scores(2 * it + 1, sb_ref)
        st = softmax_pv(2 * it, sa_ref, gmax_a, st)
        gmax_a = form_scores(2 * it + 2, sa_ref)
        st = softmax_pv(2 * it + 1, sb_ref, gmax_b, st)
        return tuple(st) + tuple(gmax_a)

    n_groups = (i + nb - 1) // nb
    res = lax.fori_loop(0, (n_groups + 1) // 2, body, tuple(state) + tuple(form_scores(0, sa_ref)))
    st = res[:6]
    o_t = jnp.concatenate([st[2] / st[1], st[5] / st[4]], axis=0)
    o_ref[...] = o_t.T


def _attn_prompt_call(q, kb, vt, ksum):
    b, s, _ = q.shape
    tq = MOBA_BLOCK
    n_blk = s // tq
    nbp = ksum.shape[1]
    nb = math.gcd(n_blk, ATTN_GROUP)
    n_pairs = ATT_W // LANES
    return pl.pallas_call(
        functools.partial(_attn_prompt_kernel, tq=tq, nbp=nbp, nb=nb, n_blk=n_blk),
        grid=(b, n_pairs, n_blk),
        in_specs=[pl.BlockSpec((None, tq, LANES), lambda bi, hp, i: (bi, i, hp)),
                  pl.BlockSpec((None, s, LANES), lambda bi, hp, i: (bi, 0, hp)),
                  pl.BlockSpec((None, n_blk, LANES, tq), lambda bi, hp, i: (bi, 0, hp, 0)),
                  pl.BlockSpec((None, nbp, LANES), lambda bi, hp, i: (bi, 0, hp))],
        out_specs=pl.BlockSpec((None, tq, LANES), lambda bi, hp, i: (bi, i, hp)),
        out_shape=jax.ShapeDtypeStruct((b, s, ATT_W), F32),
        scratch_shapes=[pltpu.VMEM((2, nbp + SUBLANES, tq), F32),
                        pltpu.VMEM((2 * nb, tq, tq), F32),
                        pltpu.VMEM((2 * nb, tq, tq), F32)],
        compiler_params=_params(3),
        name="attn_prompt",
    )(q, kb, vt, ksum)


def _attn_sample_kernel(pt_ref, q_ref, kn_ref, vn_ref, *refs, n_tok, n_blocks, tpad, page, bps):
    del pt_ref
    kp = refs[:2 * bps]
    vp = refs[2 * bps:4 * bps]
    o_ref, m_ref, l_ref, g_ref, oall_ref = refs[4 * bps:]
    n = pl.program_id(1)
    n_steps = n_blocks // bps
    rows = n_tok * ATT_HEADS
    scale = HEAD_DIM ** -0.5
    q = q_ref[...]
    qrep = jnp.concatenate([jnp.broadcast_to(q[t:t + 1], (ATT_HEADS, ATT_W)) for t in range(n_tok)], axis=0)
    head_mask = (lax.broadcasted_iota(jnp.int32, (rows, ATT_W), 0) % ATT_HEADS
                 == lax.broadcasted_iota(jnp.int32, (rows, ATT_W), 1) // HEAD_DIM)
    qs = (jnp.where(head_mask, qrep, 0.0) * scale).astype(BF16)
    lane = lax.broadcasted_iota(jnp.int32, (rows, LANES), 1)

    @pl.when(n == 0)
    def _():
        m_ref[...] = jnp.full(m_ref.shape, NEG_INF, F32)
        l_ref[...] = jnp.zeros(l_ref.shape, F32)
        g_ref[...] = jnp.zeros(g_ref.shape, F32)

    for r in range(bps):
        blk = n * bps + r
        k_t = jnp.concatenate([kp[2 * r][...].reshape(ATT_W, page), kp[2 * r + 1][...].reshape(ATT_W, page)],
                              axis=1).astype(BF16)
        v_t = jnp.concatenate([vp[2 * r][...].reshape(ATT_W, page), vp[2 * r + 1][...].reshape(ATT_W, page)],
                              axis=1).astype(BF16)
        s = jnp.dot(qs, k_t, preferred_element_type=F32)
        g_n = jnp.sum(s, axis=1, keepdims=True) * (1.0 / (scale * MOBA_BLOCK))
        m_n = jnp.max(s, axis=1, keepdims=True)
        p = jnp.exp(s - m_n)
        l_n = jnp.sum(p, axis=1, keepdims=True)
        oall_ref[blk] = lax.dot_general(p.astype(BF16), v_t, CONTRACT_LAST, preferred_element_type=F32)
        m_ref[...] = jnp.where(lane == blk, m_n, m_ref[...])
        l_ref[...] = jnp.where(lane == blk, l_n, l_ref[...])
        g_ref[...] = jnp.where(lane == blk, g_n, g_ref[...])

    @pl.when(n == n_steps - 1)
    def _():
        valid = lane < n_blocks
        sel = _top_k_mask(g_ref[...], valid, 1) > 0.0
        m_all = m_ref[...]
        l_all = l_ref[...]
        kn = kn_ref[...].astype(BF16)
        vn = vn_ref[...].astype(BF16)
        s_own = lax.dot_general(qs, kn, CONTRACT_LAST, preferred_element_type=F32)
        tq = lax.broadcasted_iota(jnp.int32, (rows, tpad), 0) // ATT_HEADS
        tk = lax.broadcasted_iota(jnp.int32, (rows, tpad), 1)
        s_own = jnp.where((tk <= tq) & (tk < n_tok), s_own, NEG_INF)
        m_own = jnp.max(s_own, axis=1, keepdims=True)
        m_fin = jnp.maximum(m_own, jnp.max(jnp.where(sel, m_all, NEG_INF), axis=1, keepdims=True))
        p_own = jnp.exp(s_own - m_fin)
        w = jnp.where(sel, jnp.exp(m_all - m_fin), 0.0)
        l_fin = jnp.sum(p_own, axis=1, keepdims=True) + jnp.sum(w * l_all, axis=1, keepdims=True)
        o = jnp.dot(p_own.astype(BF16), vn, preferred_element_type=F32)
        for nb in range(n_blocks):
            o = o + w[:, nb:nb + 1] * oall_ref[nb]
        o = jnp.where(head_mask, o / l_fin, 0.0)
        o_tok = jnp.sum(o.reshape(n_tok, ATT_HEADS, ATT_W), axis=1)
        o_ref[...] = jnp.concatenate([o_tok, jnp.zeros((tpad - n_tok, ATT_W), F32)], axis=0)


def _attn_sample_call(page_table, q, k_new, v_new, cache_kt, cache_vt, layer, n_tok):
    db, tpad, _ = q.shape
    n_pages = page_table.shape[1]
    page = cache_kt.shape[-1]
    assert MOBA_BLOCK == 2 * page, "two cache pages per MoBA block expected"
    n_blocks = n_pages // 2
    assert n_blocks <= LANES
    bps = math.gcd(n_blocks, SAMPLE_BLOCKS_PER_STEP)
    rows = n_tok * ATT_HEADS
    cpage = lambda off: pl.BlockSpec((None, None, ATT_HEADS, HEAD_DIM, page),
                                     lambda bi, n, pt, off=off: (pt[bi, 2 * bps * n + off], layer, 0, 0, 0))
    pages = [cpage(off) for off in range(2 * bps)]
    tok = pl.BlockSpec((None, tpad, ATT_W), lambda bi, n, pt: (bi, 0, 0))
    grid_spec = pltpu.PrefetchScalarGridSpec(
        num_scalar_prefetch=1,
        grid=(db, n_blocks // bps),
        in_specs=[tok, tok, tok] + pages + pages,
        out_specs=tok,
        scratch_shapes=[pltpu.VMEM((rows, LANES), F32),
                        pltpu.VMEM((rows, LANES), F32),
                        pltpu.VMEM((rows, LANES), F32),
                        pltpu.VMEM((n_blocks, rows, ATT_W), F32)],
    )
    return pl.pallas_call(
        functools.partial(_attn_sample_kernel, n_tok=n_tok, n_blocks=n_blocks, tpad=tpad, page=page, bps=bps),
        grid_spec=grid_spec,
        out_shape=jax.ShapeDtypeStruct((db, tpad, ATT_W), F32),
        compiler_params=_params(2),
        name="attn_sample",
    )(page_table, q, k_new, v_new, *([cache_kt] * (2 * bps)), *([cache_vt] * (2 * bps)))


def _gla_kernel(q_ref, f_ref, i_ref, g_ref, lbl_ref, gain_ref, s0_ref, o_ref, so_ref,
                st_ref, lc_ref, kk_ref, *, bb, tb, c, **kw):
    seqs = [_gla_sequence(q_ref.at[b], f_ref.at[b], i_ref.at[b], g_ref.at[b], lbl_ref, gain_ref, s0_ref.at[b],
                          o_ref.at[b], so_ref.at[b], st_ref.at[b], lc_ref.at[b], kk_ref.at[b], tb=tb, c=c, **kw)
            for b in range(bb)]

    def chunk(cidx, carry):
        for chunk_step, _ in seqs:
            chunk_step(cidx)
        return carry

    lax.fori_loop(0, tb // c, chunk, 0)
    for _, finish in seqs:
        finish()


def _gla_sequence(q_ref, f_ref, i_ref, g_ref, lbl_ref, gain_ref, s0_ref, o_ref, so_ref,
                  st_ref, lc_ref, kk_ref, *, layer, tb, c, t_valid, n_t):
    t = pl.program_id(1)
    half = MXU_DIM

    @pl.when(t == 0)
    def _():
        st_ref[...] = jnp.zeros(st_ref.shape, F32)
        for h in range(HG_HEADS):
            d0 = (h % 4) * HG_DK
            st_ref[h // 4, d0:d0 + HG_DK, d0:d0 + HG_DK] = s0_ref[h]

    lg = lbl_ref[...]
    e = jnp.exp(lg - jnp.max(lg, axis=0, keepdims=True))
    den = jnp.sum(e, axis=0, keepdims=True)
    lb = jnp.zeros((1, HG_W), F32)
    for j in range(1, layer + 1):
        lb = lb + e[j:j + 1] / den

    fr = f_ref[...]
    f = lb + (1.0 - lb) * _sigmoid(fr)
    logf2 = jnp.log2(f)
    kk = (1.0 - lb) * _sigmoid(-fr)
    if t_valid < tb:
        live = lax.broadcasted_iota(jnp.int32, (tb, HG_W), 0) < t_valid
        logf2 = jnp.where(live, logf2, 0.0)
        kk = jnp.where(live, kk, 0.0)
    kk_ref[...] = kk

    grp = min(LANES, tb)
    ri = lax.broadcasted_iota(jnp.int32, (grp, grp), 0)
    ci = lax.broadcasted_iota(jnp.int32, (grp, grp), 1)
    tri = jnp.where((ri // c == ci // c) & (ci <= ri), 1.0, 0.0).astype(BF16)
    for r in range(tb // grp):
        rem = logf2[r * grp:(r + 1) * grp]
        acc = None
        for _ in range(3):
            piece = rem.astype(BF16)
            rem = rem - piece.astype(F32)
            y = jnp.dot(tri, piece, preferred_element_type=F32)
            acc = y if acc is None else acc + y
        lc_ref[r * grp:(r + 1) * grp, :] = acc

    bi = lax.broadcasted_iota(jnp.int32, (half, half), 0) // HG_DK
    bj = lax.broadcasted_iota(jnp.int32, (half, half), 1) // HG_DK
    bd = jnp.where(bi == bj, 1.0, 0.0)
    bd16 = bd.astype(BF16)
    head_lanes = lax.broadcasted_iota(jnp.int32, (c, half), 1) // HG_DK

    def chunk_step(cidx):
        r0 = pl.multiple_of(cidx * c, c)
        qc = q_ref[pl.ds(r0, c), :]
        vc = i_ref[pl.ds(r0, c), :]
        kc = kk_ref[pl.ds(r0, c), :]
        lc = lc_ref[pl.ds(r0, c), :]
        last = lc[c - 1:c, :]
        qt = (qc * jnp.exp2(lc)).astype(BF16)
        o = jnp.concatenate(
            [lax.dot_general(qt[:, g * half:(g + 1) * half], st_ref[g].astype(BF16), CONTRACT_LAST,
                             preferred_element_type=F32) for g in range(2)], axis=1)
        ws = []
        for s in range(c):
            t0 = (s // SUBLANES) * SUBLANES
            dec = jnp.exp2(lc[t0:] - lc[s:s + 1, :])
            if t0:
                ws.append(jnp.zeros((t0, HG_W), F32))
            live = lax.broadcasted_iota(jnp.int32, (c - t0, HG_W), 0) >= s - t0
            ws.append(jnp.where(live, qc[t0:] * kc[s:s + 1, :] * dec, 0.0))
        w = jnp.concatenate(ws, axis=0).astype(BF16)
        att = jnp.concatenate(
            [jnp.dot(w[:, g * half:(g + 1) * half], bd16, preferred_element_type=F32) for g in range(2)], axis=1)
        tiles = [o[t0:t0 + SUBLANES] for t0 in range(0, c, SUBLANES)]
        for s in range(c):
            for ti in range(s // SUBLANES, c // SUBLANES):
                off = s * c + ti * SUBLANES
                tiles[ti] = tiles[ti] + att[off:off + SUBLANES] * vc[s:s + 1, :]
        o_ref[pl.ds(r0, c), :] = jnp.concatenate(tiles, axis=0)
        k2 = (kc * jnp.exp2(last - lc)).astype(BF16)
        vb = vc.astype(BF16)
        dl = jnp.exp2(last)
        for g in range(2):
            vg = vb[:, g * half:(g + 1) * half]
            kg = k2[:, g * half:(g + 1) * half]
            v_exp = jnp.concatenate([jnp.where(head_lanes == hh, vg, 0.0) for hh in range(4)], axis=0)
            k_exp = jnp.concatenate([jnp.where(head_lanes == hh, kg, 0.0) for hh in range(4)], axis=0)
            upd = lax.dot_general(v_exp, k_exp, CONTRACT_FIRST, preferred_element_type=F32)
            st_ref[g] = st_ref[g] * dl[:, g * half:(g + 1) * half] + upd

    def finish():
        o = o_ref[...]
        o2 = o * o
        ms = jnp.concatenate([_split_dot(o2[:, g * half:(g + 1) * half], bd, 3) for g in range(2)],
                             axis=1) * (1.0 / HG_DK)
        og = g_ref[...]
        o_ref[...] = o * lax.rsqrt(ms + RMS_EPS) * gain_ref[...] * (og * _sigmoid(og))

        @pl.when(t == n_t - 1)
        def _():
            for h in range(HG_HEADS):
                d0 = (h % 4) * HG_DK
                so_ref[h] = st_ref[h // 4, d0:d0 + HG_DK, d0:d0 + HG_DK]

    return chunk_step, finish


def _gla_call(h3, lb_logits, gain, s0_bd, layer, tb, c, t_valid, bb):
    b, s, _ = h3.shape
    n_t = s // tb
    n_layers = lb_logits.shape[0]
    assert b % bb == 0
    col = lambda cidx: pl.BlockSpec((bb, tb, COLW), lambda bi, ti, cidx=cidx: (bi, ti, cidx))
    full2 = lambda shape: pl.BlockSpec(shape, lambda bi, ti: (0, 0))
    st_spec = pl.BlockSpec((bb, HG_HEADS, HG_DK, HG_DK), lambda bi, ti: (bi, 0, 0, 0))
    return pl.pallas_call(
        functools.partial(_gla_kernel, bb=bb, layer=layer, tb=tb, c=c, t_valid=t_valid, n_t=n_t),
        grid=(b // bb, n_t),
        in_specs=[col(C_QH), col(C_FH), col(C_IH), col(C_GH),
                  full2((n_layers, HG_W)), full2((1, HG_W)), st_spec],
        out_specs=[pl.BlockSpec((bb, tb, HG_W), lambda bi, ti: (bi, ti, 0)), st_spec],
        out_shape=[jax.ShapeDtypeStruct((b, s, HG_W), F32),
                   jax.ShapeDtypeStruct((b, HG_HEADS, HG_DK, HG_DK), F32)],
        scratch_shapes=[pltpu.VMEM((bb, 2, MXU_DIM, MXU_DIM), F32),
                        pltpu.VMEM((bb, tb, HG_W), F32),
                        pltpu.VMEM((bb, tb, HG_W), F32)],
        compiler_params=_params(2),
        name="hgrn2",
    )(h3, h3, h3, h3, lb_logits, gain, s0_bd)


def _state_t(s):
    return jnp.swapaxes(s, -1, -2)


def _lru_kernel(x_ref, g_ref, cw_ref, cb_ref, wa_ref, ba_ref, wx_ref, bx_ref, lam_ref, h0_ref, buf0_ref,
                y_ref, hl_ref, bo_ref, xp_ref, a_ref, u_ref, hc_ref, *, tb, t_valid, pad, n_t):
    t = pl.program_id(1)
    nprev = CONV_W - 1
    base = SUBLANES

    @pl.when(t == 0)
    def _():
        xp_ref[base - nprev:base, :] = buf0_ref[...]
        hc_ref[...] = h0_ref[...]

    x = x_ref[...]
    xp_ref[base:base + tb, :] = x
    cw = cw_ref[...]
    xc = cb_ref[...] + cw[nprev:nprev + 1, :] * x
    for j in range(nprev):
        xc = xc + cw[j:j + 1, :] * xp_ref[base - nprev + j:base - nprev + j + tb, :]
    bo_ref[...] = xp_ref[base + t_valid - nprev:base + t_valid, :]
    xp_ref[base - nprev:base, :] = xp_ref[base + tb - nprev:base + tb, :]

    xb = xc.astype(BF16)
    r = _sigmoid(jnp.dot(xb, wa_ref[...], preferred_element_type=F32) + ba_ref[...])
    ig = _sigmoid(jnp.dot(xb, wx_ref[...], preferred_element_type=F32) + bx_ref[...])
    z = -lam_ref[...]
    softplus = jnp.maximum(z, 0.0) + jnp.log(1.0 + jnp.exp(-jnp.abs(z)))
    log_a = -LRU_C * r * softplus
    a = jnp.exp(log_a)
    u = jnp.sqrt(jnp.maximum(1.0 - jnp.exp(2.0 * log_a), SQRT_EPS)) * ig * xc
    first = lax.broadcasted_iota(jnp.int32, (tb, LRU_W), 0) == 0
    u = u + jnp.where(first, a * hc_ref[...], 0.0)

    a_ref[0:pad, :] = jnp.ones((pad, LRU_W), F32)
    u_ref[0:pad, :] = jnp.zeros((pad, LRU_W), F32)
    a_ref[pad:pad + tb, :] = a
    u_ref[pad:pad + tb, :] = u
    d = 1
    while d < tb:
        a_cur = a_ref[pad:pad + tb, :]
        u_cur = u_ref[pad:pad + tb, :]
        a_sh = a_ref[pad - d:pad - d + tb, :]
        u_sh = u_ref[pad - d:pad - d + tb, :]
        u_ref[pad:pad + tb, :] = a_cur * u_sh + u_cur
        a_ref[pad:pad + tb, :] = a_cur * a_sh
        d *= 2
    h = u_ref[pad:pad + tb, :]
    y_ref[...] = h * _gelu(g_ref[...])
    hc_ref[...] = h[tb - 1:tb, :]
    hl_ref[...] = h[t_valid - 1:t_valid, :]


def _lru_call(h3, w, h0, buf0, tb, t_valid):
    b, s, _ = h3.shape
    n_t = s // tb
    pad = max(SUBLANES, tb // 2)
    cw, cb, wa_bd, ba, wx_bd, bx, lam = w
    col = lambda cidx: pl.BlockSpec((None, tb, COLW), lambda bi, ti, cidx=cidx: (bi, ti, cidx))
    full2 = lambda shape: pl.BlockSpec(shape, lambda bi, ti: (0, 0))
    vec = full2((1, LRU_W))
    per_b = lambda r: pl.BlockSpec((None, r, LRU_W), lambda bi, ti: (bi, 0, 0))
    return pl.pallas_call(
        functools.partial(_lru_kernel, tb=tb, t_valid=t_valid, pad=pad, n_t=n_t),
        grid=(b, n_t),
        in_specs=[col(C_XL), col(C_GL), full2((CONV_W, LRU_W)), vec, full2((LRU_W, LRU_W)), vec,
                  full2((LRU_W, LRU_W)), vec, vec, per_b(1), per_b(CONV_W - 1)],
        out_specs=[pl.BlockSpec((None, tb, LRU_W), lambda bi, ti: (bi, ti, 0)), per_b(1), per_b(CONV_W - 1)],
        out_shape=[jax.ShapeDtypeStruct((b, s, LRU_W), F32),
                   jax.ShapeDtypeStruct((b, 1, LRU_W), F32),
                   jax.ShapeDtypeStruct((b, CONV_W - 1, LRU_W), F32)],
        scratch_shapes=[pltpu.VMEM((SUBLANES + tb, LRU_W), F32),
                        pltpu.VMEM((pad + tb, LRU_W), F32),
                        pltpu.VMEM((pad + tb, LRU_W), F32),
                        pltpu.VMEM((1, LRU_W), F32)],
        compiler_params=_params(2),
        name="rglru",
    )(h3, h3, cw, cb, wa_bd, ba, wx_bd, bx, lam, h0, buf0)


def _block_diag(w):
    eye = jnp.eye(LRU_BLOCKS, dtype=w.dtype).reshape(LRU_BLOCKS, 1, LRU_BLOCKS, 1)
    return (w[:, :, None, :] * eye).reshape(LRU_W, LRU_W)


def _merge_kernel(x_ref, oa_ref, oh_ref, ol_ref, g0a, g0b, g1a, g1b, g2a, g2b,
                  wa_ref, wh_ref, wl_ref, wo_ref, lg_ref, lb_ref, y_ref, *, alpha):
    def branch(o_ref, w_ref, ga, gb):
        y = jnp.dot(o_ref[...].astype(BF16), w_ref[...], preferred_element_type=F32)
        gate = jnp.concatenate([ga[...], gb[...]], axis=1)
        return _sigmoid(gate) * y

    merged = branch(oa_ref, wa_ref, g0a, g0b) + branch(oh_ref, wh_ref, g1a, g1b) + branch(ol_ref, wl_ref, g2a, g2b)
    mix = jnp.dot(merged.astype(BF16), wo_ref[...], preferred_element_type=F32)
    y_ref[...] = _layer_norm(alpha * x_ref[...] + mix, lg_ref[...], lb_ref[...])


def _merge_call(x2, h2, o_att, o_hg, o_lru, w, layer, tm, alpha):
    m = x2.shape[0]
    w_att, w_hg, w_lru, w_o, ln_g, ln_b = w
    row = lambda width: pl.BlockSpec((tm, width), lambda i: (i, 0))
    col = lambda cidx: pl.BlockSpec((tm, COLW), lambda i, cidx=cidx: (i, cidx))
    full = lambda shape: pl.BlockSpec(shape, lambda i: (0, 0))
    stacked = lambda shape: pl.BlockSpec((None,) + shape, lambda i: (layer, 0, 0))
    return pl.pallas_call(
        functools.partial(_merge_kernel, alpha=alpha),
        grid=(m // tm,),
        in_specs=[row(D_MODEL), row(ATT_W), row(HG_W), row(LRU_W)]
                 + [col(C_GM + j) for j in range(6)]
                 + [stacked((ATT_W, D_MODEL)), stacked((HG_W, D_MODEL)), stacked((LRU_W, D_MODEL)),
                    stacked((D_MODEL, D_MODEL)), full((1, D_MODEL)), full((1, D_MODEL))],
        out_specs=row(D_MODEL),
        out_shape=jax.ShapeDtypeStruct((m, D_MODEL), F32),
        compiler_params=_params(1),
        name="merge_ln",
    )(x2, o_att, o_hg, o_lru, h2, h2, h2, h2, h2, h2, w_att, w_hg, w_lru, w_o, ln_g, ln_b)


def _ffn_kernel(x_ref, wup_ref, cw_ref, cb_ref, wdn_ref, lg_ref, lb_ref, p2_ref, p1_ref,
                y_ref, tail_ref, carry_ref, *, tm, ts, fc, alpha, per_row_prev, n_t):
    i = pl.program_id(0)
    x = x_ref[...]
    xb = x.astype(BF16)
    tpos = lax.broadcasted_iota(jnp.int32, (tm, fc), 0) % ts
    cw = cw_ref[...]
    cb = cb_ref[...]

    if not per_row_prev:
        @pl.when(i % n_t == 0)
        def _():
            carry_ref[...] = p2_ref[...]

    acc = jnp.zeros((tm, D_MODEL), F32)
    for cidx in range(D_FF // fc):
        lo, hi = cidx * fc, (cidx + 1) * fc
        u = jnp.dot(xb, wup_ref[:, lo:hi], preferred_element_type=F32)
        val = jnp.dot(xb, wup_ref[:, D_FF + lo:D_FF + hi], preferred_element_type=F32)
        if per_row_prev:
            prev2 = p2_ref[:, lo:hi]
            prev1 = p1_ref[:, lo:hi]
        else:
            cm2 = carry_ref[0:1, lo:hi]
            cm1 = carry_ref[1:2, lo:hi]
            prev2 = jnp.where(tpos == 0, cm2, cm1)
            prev1 = jnp.broadcast_to(cm1, (tm, fc))
        u1 = jnp.where(tpos >= 1, pltpu.roll(u, 1, 0), prev1)
        u2 = jnp.where(tpos >= 2, pltpu.roll(u, 2, 0), prev2)
        uc = cb[:, lo:hi] + cw[0:1, lo:hi] * u2 + cw[1:2, lo:hi] * u1 + cw[2:3, lo:hi] * u
        if per_row_prev:
            tail_ref[:, lo:hi] = u
        else:
            carry_ref[:, lo:hi] = u[tm - 2:tm, :]
            tail_ref[:, lo:hi] = u[tm - SUBLANES:tm, :]
        gated = (_gelu(uc) * val).astype(BF16)
        acc = acc + jnp.dot(gated, wdn_ref[lo:hi, :], preferred_element_type=F32)
    y_ref[...] = _layer_norm(alpha * x + acc, lg_ref[...], lb_ref[...])


def _ffn_call(x2, w, layer, prev2, prev1, tm, ts, n_seq, alpha, per_row_prev):
    m = x2.shape[0]
    w_up, cw, cb, w_dn, ln_g, ln_b = w
    n_t = (m // n_seq) // tm if not per_row_prev else 1
    fc = 1024
    row = pl.BlockSpec((tm, D_MODEL), lambda i: (i, 0))
    full = lambda shape: pl.BlockSpec(shape, lambda i: (0, 0))
    resident = lambda shape: pl.BlockSpec((None,) + shape, lambda i: (layer, 0, 0), pipeline_mode=pl.Buffered(1))
    if per_row_prev:
        p2_spec = pl.BlockSpec((tm, D_FF), lambda i: (i, 0))
        p1_spec = pl.BlockSpec((tm, D_FF), lambda i: (i, 0))
        tail_spec = pl.BlockSpec((tm, D_FF), lambda i: (i, 0))
        tail_shape = jax.ShapeDtypeStruct((m, D_FF), F32)
    else:
        p2_spec = pl.BlockSpec((None, FFN_CONV_W - 1, D_FF), lambda i: (i // n_t, 0, 0))
        p1_spec = pl.BlockSpec((None, FFN_CONV_W - 1, D_FF), lambda i: (i // n_t, 0, 0))
        tail_spec = pl.BlockSpec((None, SUBLANES, D_FF), lambda i: (i // n_t, 0, 0))
        tail_shape = jax.ShapeDtypeStruct((n_seq, SUBLANES, D_FF), F32)
    return pl.pallas_call(
        functools.partial(_ffn_kernel, tm=tm, ts=ts, fc=fc, alpha=alpha, per_row_prev=per_row_prev, n_t=n_t),
        grid=(m // tm,),
        in_specs=[row, resident((D_MODEL, 2 * D_FF)), full((FFN_CONV_W, D_FF)), full((1, D_FF)),
                  resident((D_FF, D_MODEL)), full((1, D_MODEL)), full((1, D_MODEL)), p2_spec, p1_spec],
        out_specs=[row, tail_spec],
        out_shape=[jax.ShapeDtypeStruct((m, D_MODEL), F32), tail_shape],
        scratch_shapes=[pltpu.VMEM((FFN_CONV_W - 1, D_FF), F32)],
        compiler_params=_params(1),
        name="ffn_ln",
    )(x2, w_up, cw, cb, w_dn, ln_g, ln_b, prev2, prev1)


def _trunk_layer(x3, tabs, attn_fn, hg_s0, lru_h0, lru_buf0, ffn_buf0, wl, layer, cfg):
    b, s, _ = x3.shape
    m = b * s
    x2 = x3.reshape(m, D_MODEL)
    h2 = _matmul(x2, wl["w_in"], layer, cfg["bm"], cfg["bn"])
    h3 = h2.reshape(b, s, D_IN)
    rope_out = _rope_call(h3, tabs, cfg["rope_tm"], cfg["prompt"])
    k_rot, v_f32 = rope_out[1], rope_out[2]
    o_att = attn_fn(*rope_out)
    o_hg, hg_st = _gla_call(h3, wl["hg_lb_logits"], wl["hg_gain"], hg_s0, layer,
                            cfg["seq_tb"], cfg["gla_c"], cfg["t_valid"], cfg["gla_bb"])
    o_lru, lru_h, lru_buf = _lru_call(h3, wl["lru"], lru_h0, lru_buf0, cfg["seq_tb"], cfg["t_valid"])
    x1 = _merge_call(x2, h2, o_att.reshape(m, ATT_W), o_hg.reshape(m, HG_W), o_lru.reshape(m, LRU_W),
                     wl["merge"], layer, cfg["tok_tm"], cfg["alpha"])
    if cfg["prompt"]:
        x_out, tail = _ffn_call(x1, wl["ffn"], layer, ffn_buf0, ffn_buf0, cfg["tok_tm"], cfg["tok_tm"], b,
                                cfg["alpha"], False)
        ffn_buf = tail[:, SUBLANES - (FFN_CONV_W - 1):, :]
    else:
        tv = cfg["t_valid"]
        zrow = lambda n: jnp.zeros((b, n, D_FF), F32)
        prev2 = jnp.concatenate([ffn_buf0, zrow(s - 2)], axis=1).reshape(m, D_FF)
        prev1 = jnp.concatenate([ffn_buf0[:, 1:2], zrow(s - 1)], axis=1).reshape(m, D_FF)
        x_out, u_all = _ffn_call(x1, wl["ffn"], layer, prev2, prev1, m, s, b, cfg["alpha"], True)
        ffn_buf = u_all.reshape(b, s, D_FF)[:, tv - (FFN_CONV_W - 1):tv]
    return (x_out.reshape(b, s, D_MODEL), k_rot, v_f32, hg_st, lru_h.reshape(b, LRU_W), lru_buf, ffn_buf)


def kernel(x_prompt, x_sample, cache_k, cache_v, page_table, state_hgrn, state_lru_h, state_lru_conv, state_ffn_conv, w_in, hg_lb_logits, hg_gain, lru_conv_w, lru_conv_b, lru_wa, lru_ba, lru_wx, lru_bx, lru_lambda, w_br_att, w_br_hg, w_br_lru, w_o, ln1_g, ln1_b, ffn_w_up, ffn_conv_w, ffn_conv_b, ffn_w_down, ln2_g, ln2_b):
    bsz, seq, _ = x_prompt.shape
    dbs, n_tok, _ = x_sample.shape
    depth = w_in.shape[0]
    page = cache_k.shape[2]
    past = page_table.shape[1] * page
    assert seq % MOBA_BLOCK == 0 and past % MOBA_BLOCK == 0
    assert CONV_W - 1 <= n_tok <= SUBLANES
    alpha = (2.0 * depth) ** 0.25
    tpad = SUBLANES
    n_blk = seq // MOBA_BLOCK
    nbp = -(-n_blk // SUBLANES) * SUBLANES

    cache_kt = jnp.transpose(cache_k, (0, 1, 3, 4, 2))
    cache_vt = jnp.transpose(cache_v, (0, 1, 3, 4, 2))
    tabs_p = _rope_tables(jnp.arange(seq))
    tabs_s = _rope_tables(past + jnp.arange(tpad))

    seq_tb = min(512, seq)
    assert seq % seq_tb == 0 and (bsz * seq) % min(1024, bsz * seq) == 0
    cfg_p = dict(prompt=True, bm=min(1024, bsz * seq), bn=1536, rope_tm=min(512, seq), seq_tb=seq_tb,
                 gla_c=16, gla_bb=math.gcd(bsz, 2), t_valid=seq_tb, tok_tm=min(512, seq), alpha=alpha)
    cfg_s = dict(prompt=False, bm=dbs * tpad, bn=1536, rope_tm=tpad, seq_tb=tpad,
                 gla_c=tpad, gla_bb=math.gcd(dbs, 4), t_valid=n_tok, tok_tm=dbs * tpad, alpha=alpha)

    xp = x_prompt
    xs = jnp.pad(x_sample, ((0, 0), (0, tpad - n_tok), (0, 0)))
    hg0_p = jnp.zeros((bsz, HG_HEADS, HG_DK, HG_DK), F32)
    lh0_p = jnp.zeros((bsz, 1, LRU_W), F32)
    lb0_p = jnp.zeros((bsz, CONV_W - 1, LRU_W), F32)
    fb0_p = jnp.zeros((bsz, FFN_CONV_W - 1, D_FF), F32)

    def attn_p(q_rot, k_rot, v_f32, k_bf, v_t, ksum):
        ks = jnp.pad(ksum.reshape(bsz, n_blk, ATT_W), ((0, 0), (0, nbp - n_blk), (0, 0)))
        return _attn_prompt_call(q_rot, k_bf, v_t, ks)

    w_in_bf, w_att_bf, w_hg_bf, w_lru_bf, w_o_bf, w_up_bf, w_dn_bf = (
        w.astype(BF16) for w in (w_in, w_br_att, w_br_hg, w_br_lru, w_o, ffn_w_up, ffn_w_down))

    outs_p = [[] for _ in range(6)]
    outs_s = [[] for _ in range(6)]
    row = lambda a: a.reshape(1, -1)
    for l in range(depth):
        wl = dict(
            w_in=w_in_bf,
            hg_lb_logits=hg_lb_logits,
            hg_gain=row(hg_gain[l]),
            lru=(lru_conv_w[l], row(lru_conv_b[l]), _block_diag(lru_wa[l]).astype(BF16), row(lru_ba[l]),
                 _block_diag(lru_wx[l]).astype(BF16), row(lru_bx[l]), row(lru_lambda[l])),
            merge=(w_att_bf, w_hg_bf, w_lru_bf, w_o_bf, row(ln1_g[l]), row(ln1_b[l])),
            ffn=(w_up_bf, ffn_conv_w[l], row(ffn_conv_b[l]), w_dn_bf, row(ln2_g[l]), row(ln2_b[l])),
        )

        def attn_s(q_rot, k_rot, v_f32, l=l):
            return _attn_sample_call(page_table, q_rot, k_rot, v_f32, cache_kt, cache_vt, l, n_tok)

        xp, *new_p = _trunk_layer(xp, tabs_p, attn_p, hg0_p, lh0_p, lb0_p, fb0_p, wl, l, cfg_p)
        xs, *new_s = _trunk_layer(xs, tabs_s, attn_s, _state_t(state_hgrn[:, l]),
                                  state_lru_h[:, l].reshape(dbs, 1, LRU_W), state_lru_conv[:, l],
                                  state_ffn_conv[:, l], wl, l, cfg_s)
        for lst, a in zip(outs_p, new_p):
            lst.append(a)
        for lst, a in zip(outs_s, new_s):
            lst.append(a)

    def heads(a, t):
        return a.reshape(a.shape[0], a.shape[1], t, ATT_HEADS, HEAD_DIM)

    def heads_t(a):
        return jnp.transpose(a.reshape(bsz, depth, ATT_HEADS, HEAD_DIM, seq), (0, 1, 4, 2, 3))

    k_p = heads_t(jnp.stack(outs_p[0], axis=1))
    v_p = heads_t(jnp.stack(outs_p[1], axis=1))
    k_s = heads(jnp.stack([a[:, :n_tok] for a in outs_s[0]], axis=1), n_tok)
    v_s = heads(jnp.stack([a[:, :n_tok] for a in outs_s[1]], axis=1), n_tok)
    hg_p = _state_t(jnp.stack(outs_p[2], axis=1))
    hg_s = _state_t(jnp.stack(outs_s[2], axis=1))
    lh_p = jnp.stack(outs_p[3], axis=1)
    lh_s = jnp.stack(outs_s[3], axis=1)
    lc_p = jnp.stack(outs_p[4], axis=1)
    lc_s = jnp.stack(outs_s[4], axis=1)
    fc_p = jnp.stack(outs_p[5], axis=1)
    fc_s = jnp.stack(outs_s[5], axis=1)
    return (xp, xs[:, :n_tok], k_p, v_p, k_s, v_s, hg_p, hg_s, lh_p, lh_s, lc_p, lc_s, fc_p, fc_s)
```

```python
import functools
import math

import jax
import jax.numpy as jnp
from jax import lax
from jax.experimental import pallas as pl
from jax.experimental.pallas import tpu as pltpu

F32 = jnp.float32
BF16 = jnp.bfloat16

D_MODEL = 1024
ATT_HEADS = 8
HEAD_DIM = 64
ATT_W = ATT_HEADS * HEAD_DIM
MOBA_BLOCK = 256
MOBA_TOPK = 3
ROPE_THETA = 500000.0
ROPE_DIM = HEAD_DIM // 4
HG_HEADS = 8
HG_DK = 64
HG_W = HG_HEADS * HG_DK
LRU_W = 512
LRU_BLOCKS = 8
LRU_BD = LRU_W // LRU_BLOCKS
LRU_C = 8.0
CONV_W = 4
D_FF = 3 * D_MODEL
FFN_CONV_W = 3
N_BRANCH = 3
LN_EPS = 1e-5
RMS_EPS = 1e-6
SQRT_EPS = 1e-12
NEG_INF = -1e30
D_IN = 3 * ATT_W + 4 * HG_W + 2 * LRU_W + N_BRANCH * D_MODEL
LOG2E = math.log2(math.e)
ATTN_GROUP = 2
SAMPLE_BLOCKS_PER_STEP = 4

COLW = 512
C_QA, C_KA, C_VA, C_QH, C_FH, C_IH, C_XL = 0, 1, 2, 3, 4, 5, 6
C_GH, C_GL, C_GM = 0, 1, 2
HF_W = 7 * COLW
HB_W = 2 * COLW + N_BRANCH * D_MODEL

LANES = 128
SUBLANES = 8
MXU_DIM = 256
VMEM_LIMIT = 56 * 1024 * 1024

HIGHEST = lax.Precision.HIGHEST
CONTRACT_LAST = (((1,), (1,)), ((), ()))
CONTRACT_FIRST = (((0,), (0,)), ((), ()))


def _params(n_grid):
    return pltpu.CompilerParams(dimension_semantics=("arbitrary",) * n_grid,
                                vmem_limit_bytes=VMEM_LIMIT)


def _sigmoid(x):
    return 1.0 / (1.0 + jnp.exp(-x))


def _gelu(x):
    c = math.sqrt(2.0 / math.pi)
    return x * (0.5 * (1.0 + jnp.tanh(c * (x + 0.044715 * (x * x * x)))))


def _layer_norm(y, g, b):
    mu = jnp.mean(y, axis=-1, keepdims=True)
    d = y - mu
    var = jnp.mean(d * d, axis=-1, keepdims=True)
    return d * lax.rsqrt(var + LN_EPS) * g + b


def _split_dot(x, w01, parts):
    w = w01.astype(BF16)
    acc = None
    rem = x
    for _ in range(parts):
        piece = rem.astype(BF16)
        rem = rem - piece.astype(F32)
        y = jnp.dot(piece, w, preferred_element_type=F32)
        acc = y if acc is None else acc + y
    return acc


def _top_k_mask(gate, valid, axis):
    idx_all = lax.broadcasted_iota(jnp.int32, gate.shape, axis).astype(F32)
    g = jnp.where(valid, gate, -jnp.inf)
    sel = jnp.zeros(gate.shape, F32)
    for _ in range(MOBA_TOPK):
        mx = jnp.max(g, axis=axis, keepdims=True)
        idx = jnp.min(jnp.where(g == mx, idx_all, float(gate.shape[axis])), axis=axis, keepdims=True)
        pick = idx_all == idx
        sel = jnp.where(pick, 1.0, sel)
        g = jnp.where(pick, -jnp.inf, g)
    return jnp.where(valid, sel, 0.0)


def _mm_kernel(x_ref, wf_ref, wb_ref, of_ref, ob_ref, *, nf):
    j = pl.program_id(1)
    xb = x_ref[...].astype(BF16)

    @pl.when(j < nf)
    def _():
        of_ref[...] = jnp.dot(xb, wf_ref[...], preferred_element_type=F32)

    @pl.when(j >= nf)
    def _():
        ob_ref[...] = jnp.dot(xb, wb_ref[...], preferred_element_type=F32).astype(BF16)


def _matmul(x, wf, wb, layer, bm):
    m, k = x.shape
    bnf, bnb = HF_W // 2, HB_W // 2
    nf, nb = HF_W // bnf, HB_W // bnb
    assert m % bm == 0
    jf = lambda j: jnp.minimum(j, nf - 1)
    jb = lambda j: jnp.maximum(j - nf, 0)
    return pl.pallas_call(
        functools.partial(_mm_kernel, nf=nf),
        grid=(m // bm, nf + nb),
        in_specs=[pl.BlockSpec((bm, k), lambda i, j: (i, 0)),
                  pl.BlockSpec((None, k, bnf), lambda i, j: (layer, 0, jf(j))),
                  pl.BlockSpec((None, k, bnb), lambda i, j: (layer, 0, jb(j)))],
        out_specs=[pl.BlockSpec((bm, bnf), lambda i, j: (i, jf(j))),
                   pl.BlockSpec((bm, bnb), lambda i, j: (i, jb(j)))],
        out_shape=[jax.ShapeDtypeStruct((m, HF_W), F32), jax.ShapeDtypeStruct((m, HB_W), BF16)],
        compiler_params=_params(2),
        name="in_proj",
    )(x, wf, wb)


def _rope_kernel(q_ref, k_ref, v_ref, c_ref, sa_ref, sb_ref, qo_ref, ko_ref, vo_ref, *extra, n_blk):
    c = c_ref[...]
    sa = sa_ref[...]
    sb = sb_ref[...]

    def rot(x):
        outs = []
        for j in range(ATT_W // LANES):
            xs = x[:, LANES * j:LANES * (j + 1)]
            outs.append(xs * c + pltpu.roll(xs, ROPE_DIM // 2, 1) * sa
                        + pltpu.roll(xs, LANES - ROPE_DIM // 2, 1) * sb)
        return jnp.concatenate(outs, axis=1)

    q = rot(q_ref[...])
    k = rot(k_ref[...])
    v = v_ref[...]
    qo_ref[...] = q
    if n_blk:
        kb_ref, vt_ref, ks_ref = extra
        kb_ref[...] = k.astype(BF16)
        v_t = v.T
        ko_ref[...] = k.T
        vo_ref[...] = v_t
        for r in range(n_blk):
            rows = slice(MOBA_BLOCK * r, MOBA_BLOCK * (r + 1))
            ks_ref[r] = jnp.sum(k[rows], axis=0, keepdims=True)
            vt_ref[r] = v_t[:, rows].astype(BF16)
    else:
        ko_ref[...] = k
        vo_ref[...] = v


def _rope_call(h3, tabs, tm, prompt):
    b, s, _ = h3.shape
    nt = s // tm
    n_blk = tm // MOBA_BLOCK if prompt else 0
    col = lambda cidx: pl.BlockSpec((None, tm, COLW), lambda bi, ti, cidx=cidx: (bi, ti, cidx))
    tab = pl.BlockSpec((tm, LANES), lambda bi, ti: (ti, 0))
    row = pl.BlockSpec((None, tm, ATT_W), lambda bi, ti: (bi, ti, 0))
    out_specs = [row, row, row]
    out_shape = [jax.ShapeDtypeStruct((b, s, ATT_W), F32)] * 3
    if prompt:
        row_t = pl.BlockSpec((None, ATT_W, tm), lambda bi, ti: (bi, 0, ti))
        out_specs = [row, row_t, row_t]
        out_shape = [jax.ShapeDtypeStruct((b, s, ATT_W), F32)] + [jax.ShapeDtypeStruct((b, ATT_W, s), F32)] * 2
        out_specs += [row,
                      pl.BlockSpec((None, n_blk, ATT_W, MOBA_BLOCK), lambda bi, ti: (bi, ti, 0, 0)),
                      pl.BlockSpec((None, n_blk, 1, ATT_W), lambda bi, ti: (bi, ti, 0, 0))]
        out_shape += [jax.ShapeDtypeStruct((b, s, ATT_W), BF16),
                      jax.ShapeDtypeStruct((b, s // MOBA_BLOCK, ATT_W, MOBA_BLOCK), BF16),
                      jax.ShapeDtypeStruct((b, s // MOBA_BLOCK, 1, ATT_W), F32)]
    return pl.pallas_call(
        functools.partial(_rope_kernel, n_blk=n_blk),
        grid=(b, nt),
        in_specs=[col(C_QA), col(C_KA), col(C_VA), tab, tab, tab],
        out_specs=out_specs,
        out_shape=out_shape,
        compiler_params=_params(2),
        name="rope",
    )(h3, h3, h3, *tabs)


def _rope_tables(pos):
    half = ROPE_DIM // 2
    inv = ROPE_THETA ** (-jnp.arange(half, dtype=F32) / half)
    ang = pos.astype(F32)[:, None] * inv[None, :]
    cos, sin = jnp.cos(ang), jnp.sin(ang)
    lh = jnp.arange(LANES) % HEAD_DIM
    fi = lh % half
    c = jnp.where(lh[None, :] < ROPE_DIM, cos[:, fi], 1.0)
    sa = jnp.where((lh[None, :] >= half) & (lh[None, :] < ROPE_DIM), sin[:, fi], 0.0)
    sb = jnp.where(lh[None, :] < half, -sin[:, fi], 0.0)
    return c.astype(F32), sa.astype(F32), sb.astype(F32)


def _attn_prompt_kernel(q_ref, k_ref, vt_ref, ks_ref, o_ref, bias_ref, sa_ref, sb_ref, *, tq, nbp, nb, n_blk):
    i = pl.program_id(2)
    qscale = HEAD_DIM ** -0.5 * LOG2E
    q = q_ref[...]
    kmean = ks_ref[...] * (1.0 / MOBA_BLOCK)
    lane = lax.broadcasted_iota(jnp.int32, (tq, LANES), 1)
    valid = lax.broadcasted_iota(jnp.int32, (nbp, tq), 0) < i
    causal = (lax.broadcasted_iota(jnp.int32, (tq, tq), 0) <= lax.broadcasted_iota(jnp.int32, (tq, tq), 1))
    own0 = pl.multiple_of(i * tq, tq)
    k_own = k_ref[pl.ds(own0, tq), :]
    vt_own = vt_ref[i]

    qs = []
    state = []
    for hh in range(2):
        in_head = (lane >= hh * HEAD_DIM) & (lane < (hh + 1) * HEAD_DIM)
        qh = jnp.where(in_head, q, 0.0)
        gate_t = lax.dot_general(kmean, qh, CONTRACT_LAST, precision=HIGHEST, preferred_element_type=F32)
        sel = _top_k_mask(gate_t, valid, 0)
        bias_ref[hh, 0:nbp, :] = jnp.where(sel > 0.0, 0.0, NEG_INF)
        qsh = (qh * qscale).astype(BF16)
        qs.append(qsh)
        s = lax.dot_general(k_own, qsh, CONTRACT_LAST, preferred_element_type=F32)
        s = jnp.where(causal, s, NEG_INF)
        m = jnp.max(s, axis=0, keepdims=True)
        p = jnp.exp2(s - m)
        l = jnp.sum(p, axis=0, keepdims=True)
        acc = jnp.dot(vt_own[hh * HEAD_DIM:(hh + 1) * HEAD_DIM, :], p.astype(BF16), preferred_element_type=F32)
        state += [m, l, acc]

    neg_rows = jnp.full((SUBLANES, tq), NEG_INF, F32)
    bias_ref[0, nbp:nbp + SUBLANES, :] = neg_rows
    bias_ref[1, nbp:nbp + SUBLANES, :] = neg_rows

    def form_scores(jg, s_ref):
        gmax = [None, None]
        for r in range(nb):
            j = jg * nb + r
            jk = jnp.minimum(j, n_blk - 1)
            jb = jnp.minimum(j, nbp)
            kj = k_ref[pl.ds(pl.multiple_of(jk * tq, tq), tq), :]
            for hh in range(2):
                s = (lax.dot_general(kj, qs[hh], CONTRACT_LAST, preferred_element_type=F32)
                     + bias_ref[hh, pl.ds(jb, 1), :])
                s_ref[r * 2 + hh] = s
                mx = jnp.max(s, axis=0, keepdims=True)
                gmax[hh] = mx if gmax[hh] is None else jnp.maximum(gmax[hh], mx)
        return gmax

    def softmax_pv(jg, s_ref, gmax, st):
        out = []
        for hh in range(2):
            m, l, acc = st[3 * hh:3 * hh + 3]
            m_new = jnp.maximum(m, gmax[hh])
            alpha = jnp.exp2(m - m_new)
            l = alpha * l
            acc = alpha * acc
            for r in range(nb):
                vtj = vt_ref[jnp.minimum(jg * nb + r, n_blk - 1)]
                p = jnp.exp2(s_ref[r * 2 + hh] - m_new)
                l = l + jnp.sum(p, axis=0, keepdims=True)
                acc = acc + jnp.dot(vtj[hh * HEAD_DIM:(hh + 1) * HEAD_DIM, :], p.astype(BF16),
                                    preferred_element_type=F32)
            out += [m_new, l, acc]
        return out

    def body(it, carry):
        st, gmax_a = list(carry[:6]), list(carry[6:])
        gmax_b = form_scores(2 * it + 1, sb_ref)
        st = softmax_pv(2 * it, sa_ref, gmax_a, st)
        gmax_a = form_scores(2 * it + 2, sa_ref)
        st = softmax_pv(2 * it + 1, sb_ref, gmax_b, st)
        return tuple(st) + tuple(gmax_a)

    n_groups = (i + nb - 1) // nb
    res = lax.fori_loop(0, (n_groups + 1) // 2, body, tuple(state) + tuple(form_scores(0, sa_ref)))
    st = res[:6]
    o_t = jnp.concatenate([st[2] / st[1], st[5] / st[4]], axis=0)
    o_ref[...] = o_t.T.astype(o_ref.dtype)


def _attn_prompt_call(q, kb, vt, ksum):
    b, s, _ = q.shape
    tq = MOBA_BLOCK
    n_blk = s // tq
    nbp = ksum.shape[1]
    nb = math.gcd(n_blk, ATTN_GROUP)
    n_pairs = ATT_W // LANES
    return pl.pallas_call(
        functools.partial(_attn_prompt_kernel, tq=tq, nbp=nbp, nb=nb, n_blk=n_blk),
        grid=(b, n_pairs, n_blk),
        in_specs=[pl.BlockSpec((None, tq, LANES), lambda bi, hp, i: (bi, i, hp)),
                  pl.BlockSpec((None, s, LANES), lambda bi, hp, i: (bi, 0, hp)),
                  pl.BlockSpec((None, n_blk, LANES, tq), lambda bi, hp, i: (bi, 0, hp, 0)),
                  pl.BlockSpec((None, nbp, LANES), lambda bi, hp, i: (bi, 0, hp))],
        out_specs=pl.BlockSpec((None, tq, LANES), lambda bi, hp, i: (bi, i, hp)),
        out_shape=jax.ShapeDtypeStruct((b, s, ATT_W), BF16),
        scratch_shapes=[pltpu.VMEM((2, nbp + SUBLANES, tq), F32),
                        pltpu.VMEM((2 * nb, tq, tq), F32),
                        pltpu.VMEM((2 * nb, tq, tq), F32)],
        compiler_params=_params(3),
        name="attn_prompt",
    )(q, kb, vt, ksum)


def _attn_sample_kernel(pt_ref, q_ref, kn_ref, vn_ref, *refs, n_tok, n_blocks, tpad, page, bps):
    del pt_ref
    kp = refs[:2 * bps]
    vp = refs[2 * bps:4 * bps]
    o_ref, m_ref, l_ref, g_ref, oall_ref = refs[4 * bps:]
    n = pl.program_id(1)
    n_steps = n_blocks // bps
    rows = n_tok * ATT_HEADS
    scale = HEAD_DIM ** -0.5
    q = q_ref[...]
    qrep = jnp.concatenate([jnp.broadcast_to(q[t:t + 1], (ATT_HEADS, ATT_W)) for t in range(n_tok)], axis=0)
    head_mask = (lax.broadcasted_iota(jnp.int32, (rows, ATT_W), 0) % ATT_HEADS
                 == lax.broadcasted_iota(jnp.int32, (rows, ATT_W), 1) // HEAD_DIM)
    qs = (jnp.where(head_mask, qrep, 0.0) * scale).astype(BF16)
    lane = lax.broadcasted_iota(jnp.int32, (rows, LANES), 1)

    @pl.when(n == 0)
    def _():
        m_ref[...] = jnp.full(m_ref.shape, NEG_INF, F32)
        l_ref[...] = jnp.zeros(l_ref.shape, F32)
        g_ref[...] = jnp.zeros(g_ref.shape, F32)

    for r in range(bps):
        blk = n * bps + r
        k_t = jnp.concatenate([kp[2 * r][...].reshape(ATT_W, page), kp[2 * r + 1][...].reshape(ATT_W, page)],
                              axis=1).astype(BF16)
        v_t = jnp.concatenate([vp[2 * r][...].reshape(ATT_W, page), vp[2 * r + 1][...].reshape(ATT_W, page)],
                              axis=1).astype(BF16)
        s = jnp.dot(qs, k_t, preferred_element_type=F32)
        g_n = jnp.sum(s, axis=1, keepdims=True) * (1.0 / (scale * MOBA_BLOCK))
        m_n = jnp.max(s, axis=1, keepdims=True)
        p = jnp.exp(s - m_n)
        l_n = jnp.sum(p, axis=1, keepdims=True)
        oall_ref[blk] = lax.dot_general(p.astype(BF16), v_t, CONTRACT_LAST, preferred_element_type=F32)
        m_ref[...] = jnp.where(lane == blk, m_n, m_ref[...])
        l_ref[...] = jnp.where(lane == blk, l_n, l_ref[...])
        g_ref[...] = jnp.where(lane == blk, g_n, g_ref[...])

    @pl.when(n == n_steps - 1)
    def _():
        valid = lane < n_blocks
        sel = _top_k_mask(g_ref[...], valid, 1) > 0.0
        m_all = m_ref[...]
        l_all = l_ref[...]
        kn = kn_ref[...].astype(BF16)
        vn = vn_ref[...].astype(BF16)
        s_own = lax.dot_general(qs, kn, CONTRACT_LAST, preferred_element_type=F32)
        tq = lax.broadcasted_iota(jnp.int32, (rows, tpad), 0) // ATT_HEADS
        tk = lax.broadcasted_iota(jnp.int32, (rows, tpad), 1)
        s_own = jnp.where((tk <= tq) & (tk < n_tok), s_own, NEG_INF)
        m_own = jnp.max(s_own, axis=1, keepdims=True)
        m_fin = jnp.maximum(m_own, jnp.max(jnp.where(sel, m_all, NEG_INF), axis=1, keepdims=True))
        p_own = jnp.exp(s_own - m_fin)
        w = jnp.where(sel, jnp.exp(m_all - m_fin), 0.0)
        l_fin = jnp.sum(p_own, axis=1, keepdims=True) + jnp.sum(w * l_all, axis=1, keepdims=True)
        o = jnp.dot(p_own.astype(BF16), vn, preferred_element_type=F32)
        for nb in range(n_blocks):
            o = o + w[:, nb:nb + 1] * oall_ref[nb]
        o = jnp.where(head_mask, o / l_fin, 0.0)
        o_tok = jnp.sum(o.reshape(n_tok, ATT_HEADS, ATT_W), axis=1)
        o_ref[...] = jnp.concatenate([o_tok, jnp.zeros((tpad - n_tok, ATT_W), F32)], axis=0).astype(o_ref.dtype)


def _attn_sample_call(page_table, q, k_new, v_new, cache_kt, cache_vt, layer, n_tok):
    db, tpad, _ = q.shape
    n_pages = page_table.shape[1]
    page = cache_kt.shape[-1]
    assert MOBA_BLOCK == 2 * page, "two cache pages per MoBA block expected"
    n_blocks = n_pages // 2
    assert n_blocks <= LANES
    bps = math.gcd(n_blocks, SAMPLE_BLOCKS_PER_STEP)
    rows = n_tok * ATT_HEADS
    cpage = lambda off: pl.BlockSpec((None, None, ATT_HEADS, HEAD_DIM, page),
                                     lambda bi, n, pt, off=off: (pt[bi, 2 * bps * n + off], layer, 0, 0, 0))
    pages = [cpage(off) for off in range(2 * bps)]
    tok = pl.BlockSpec((None, tpad, ATT_W), lambda bi, n, pt: (bi, 0, 0))
    grid_spec = pltpu.PrefetchScalarGridSpec(
        num_scalar_prefetch=1,
        grid=(db, n_blocks // bps),
        in_specs=[tok, tok, tok] + pages + pages,
        out_specs=tok,
        scratch_shapes=[pltpu.VMEM((rows, LANES), F32),
                        pltpu.VMEM((rows, LANES), F32),
                        pltpu.VMEM((rows, LANES), F32),
                        pltpu.VMEM((n_blocks, rows, ATT_W), F32)],
    )
    return pl.pallas_call(
        functools.partial(_attn_sample_kernel, n_tok=n_tok, n_blocks=n_blocks, tpad=tpad, page=page, bps=bps),
        grid_spec=grid_spec,
        out_shape=jax.ShapeDtypeStruct((db, tpad, ATT_W), BF16),
        compiler_params=_params(2),
        name="attn_sample",
    )(page_table, q, k_new, v_new, *([cache_kt] * (2 * bps)), *([cache_vt] * (2 * bps)))


def _gla_kernel(q_ref, f_ref, i_ref, g_ref, lbl_ref, gain_ref, s0_ref, o_ref, so_ref,
                st_ref, lc_ref, kk_ref, acc_ref, *, bb, tb, c, **kw):
    seqs = [_gla_sequence(q_ref.at[b], f_ref.at[b], i_ref.at[b], g_ref.at[b], lbl_ref, gain_ref, s0_ref.at[b],
                          o_ref.at[b], so_ref.at[b], st_ref.at[b], lc_ref.at[b], kk_ref.at[b], acc_ref.at[b],
                          tb=tb, c=c, **kw)
            for b in range(bb)]

    def chunk(cidx, carry):
        for chunk_step, _ in seqs:
            chunk_step(cidx)
        return carry

    lax.fori_loop(0, tb // c, chunk, 0)
    for _, finish in seqs:
        finish()


def _gla_sequence(q_ref, f_ref, i_ref, g_ref, lbl_ref, gain_ref, s0_ref, o_ref, so_ref,
                  st_ref, lc_ref, kk_ref, acc_ref, *, layer, tb, c, t_valid, n_t):
    t = pl.program_id(1)
    half = MXU_DIM

    @pl.when(t == 0)
    def _():
        st_ref[...] = jnp.zeros(st_ref.shape, F32)
        for h in range(HG_HEADS):
            d0 = (h % 4) * HG_DK
            st_ref[h // 4, d0:d0 + HG_DK, d0:d0 + HG_DK] = s0_ref[h]

    lg = lbl_ref[...]
    e = jnp.exp(lg - jnp.max(lg, axis=0, keepdims=True))
    den = jnp.sum(e, axis=0, keepdims=True)
    lb = jnp.zeros((1, HG_W), F32)
    for j in range(1, layer + 1):
        lb = lb + e[j:j + 1] / den

    fr = f_ref[...]
    f = lb + (1.0 - lb) * _sigmoid(fr)
    logf2 = jnp.log2(f)
    kk = (1.0 - lb) * _sigmoid(-fr)
    if t_valid < tb:
        live = lax.broadcasted_iota(jnp.int32, (tb, HG_W), 0) < t_valid
        logf2 = jnp.where(live, logf2, 0.0)
        kk = jnp.where(live, kk, 0.0)
    kk_ref[...] = kk

    grp = min(LANES, tb)
    ri = lax.broadcasted_iota(jnp.int32, (grp, grp), 0)
    ci = lax.broadcasted_iota(jnp.int32, (grp, grp), 1)
    tri = jnp.where((ri // c == ci // c) & (ci <= ri), 1.0, 0.0).astype(BF16)
    for r in range(tb // grp):
        rem = logf2[r * grp:(r + 1) * grp]
        acc = None
        for _ in range(3):
            piece = rem.astype(BF16)
            rem = rem - piece.astype(F32)
            y = jnp.dot(tri, piece, preferred_element_type=F32)
            acc = y if acc is None else acc + y
        lc_ref[r * grp:(r + 1) * grp, :] = acc

    bi = lax.broadcasted_iota(jnp.int32, (half, half), 0) // HG_DK
    bj = lax.broadcasted_iota(jnp.int32, (half, half), 1) // HG_DK
    bd = jnp.where(bi == bj, 1.0, 0.0)
    bd16 = bd.astype(BF16)
    head_lanes = lax.broadcasted_iota(jnp.int32, (c, half), 1) // HG_DK

    def chunk_step(cidx):
        r0 = pl.multiple_of(cidx * c, c)
        qc = q_ref[pl.ds(r0, c), :]
        vc = i_ref[pl.ds(r0, c), :]
        kc = kk_ref[pl.ds(r0, c), :]
        lc = lc_ref[pl.ds(r0, c), :]
        last = lc[c - 1:c, :]
        qt = (qc * jnp.exp2(lc)).astype(BF16)
        o = jnp.concatenate(
            [lax.dot_general(qt[:, g * half:(g + 1) * half], st_ref[g].astype(BF16), CONTRACT_LAST,
                             preferred_element_type=F32) for g in range(2)], axis=1)
        ws = []
        for s in range(c):
            t0 = (s // SUBLANES) * SUBLANES
            dec = jnp.exp2(lc[t0:] - lc[s:s + 1, :])
            if t0:
                ws.append(jnp.zeros((t0, HG_W), F32))
            live = lax.broadcasted_iota(jnp.int32, (c - t0, HG_W), 0) >= s - t0
            ws.append(jnp.where(live, qc[t0:] * kc[s:s + 1, :] * dec, 0.0))
        w = jnp.concatenate(ws, axis=0).astype(BF16)
        att = jnp.concatenate(
            [jnp.dot(w[:, g * half:(g + 1) * half], bd16, preferred_element_type=F32) for g in range(2)], axis=1)
        tiles = [o[t0:t0 + SUBLANES] for t0 in range(0, c, SUBLANES)]
        for s in range(c):
            for ti in range(s // SUBLANES, c // SUBLANES):
                off = s * c + ti * SUBLANES
                tiles[ti] = tiles[ti] + att[off:off + SUBLANES] * vc[s:s + 1, :]
        acc_ref[pl.ds(r0, c), :] = jnp.concatenate(tiles, axis=0)
        k2 = (kc * jnp.exp2(last - lc)).astype(BF16)
        vb = vc.astype(BF16)
        dl = jnp.exp2(last)
        for g in range(2):
            vg = vb[:, g * half:(g + 1) * half]
            kg = k2[:, g * half:(g + 1) * half]
            v_exp = jnp.concatenate([jnp.where(head_lanes == hh, vg, 0.0) for hh in range(4)], axis=0)
            k_exp = jnp.concatenate([jnp.where(head_lanes == hh, kg, 0.0) for hh in range(4)], axis=0)
            upd = lax.dot_general(v_exp, k_exp, CONTRACT_FIRST, preferred_element_type=F32)
            st_ref[g] = st_ref[g] * dl[:, g * half:(g + 1) * half] + upd

    def finish():
        o = acc_ref[...]
        o2 = o * o
        ms = jnp.concatenate([_split_dot(o2[:, g * half:(g + 1) * half], bd, 3) for g in range(2)],
                             axis=1) * (1.0 / HG_DK)
        og = g_ref[...].astype(F32)
        o_ref[...] = (o * lax.rsqrt(ms + RMS_EPS) * gain_ref[...] * (og * _sigmoid(og))).astype(o_ref.dtype)

        @pl.when(t == n_t - 1)
        def _():
            for h in range(HG_HEADS):
                d0 = (h % 4) * HG_DK
                so_ref[h] = st_ref[h // 4, d0:d0 + HG_DK, d0:d0 + HG_DK]

    return chunk_step, finish


def _gla_call(hf3, hb3, lb_logits, gain, s0_bd, layer, tb, c, t_valid, bb):
    b, s, _ = hf3.shape
    n_t = s // tb
    n_layers = lb_logits.shape[0]
    assert b % bb == 0
    col = lambda cidx: pl.BlockSpec((bb, tb, COLW), lambda bi, ti, cidx=cidx: (bi, ti, cidx))
    full2 = lambda shape: pl.BlockSpec(shape, lambda bi, ti: (0, 0))
    st_spec = pl.BlockSpec((bb, HG_HEADS, HG_DK, HG_DK), lambda bi, ti: (bi, 0, 0, 0))
    return pl.pallas_call(
        functools.partial(_gla_kernel, bb=bb, layer=layer, tb=tb, c=c, t_valid=t_valid, n_t=n_t),
        grid=(b // bb, n_t),
        in_specs=[col(C_QH), col(C_FH), col(C_IH), col(C_GH),
                  full2((n_layers, HG_W)), full2((1, HG_W)), st_spec],
        out_specs=[pl.BlockSpec((bb, tb, HG_W), lambda bi, ti: (bi, ti, 0)), st_spec],
        out_shape=[jax.ShapeDtypeStruct((b, s, HG_W), BF16),
                   jax.ShapeDtypeStruct((b, HG_HEADS, HG_DK, HG_DK), F32)],
        scratch_shapes=[pltpu.VMEM((bb, 2, MXU_DIM, MXU_DIM), F32),
                        pltpu.VMEM((bb, tb, HG_W), F32),
                        pltpu.VMEM((bb, tb, HG_W), F32),
                        pltpu.VMEM((bb, tb, HG_W), F32)],
        compiler_params=_params(2),
        name="hgrn2",
    )(hf3, hf3, hf3, hb3, lb_logits, gain, s0_bd)


def _state_t(s):
    return jnp.swapaxes(s, -1, -2)


def _lru_kernel(x_ref, g_ref, cw_ref, cb_ref, wa_ref, ba_ref, wx_ref, bx_ref, lam_ref, h0_ref, buf0_ref,
                y_ref, hl_ref, bo_ref, xp_ref, a_ref, u_ref, hc_ref, *, tb, t_valid, pad, n_t):
    t = pl.program_id(1)
    nprev = CONV_W - 1
    base = SUBLANES

    @pl.when(t == 0)
    def _():
        xp_ref[base - nprev:base, :] = buf0_ref[...]
        hc_ref[...] = h0_ref[...]

    x = x_ref[...]
    xp_ref[base:base + tb, :] = x
    cw = cw_ref[...]
    xc = cb_ref[...] + cw[nprev:nprev + 1, :] * x
    for j in range(nprev):
        xc = xc + cw[j:j + 1, :] * xp_ref[base - nprev + j:base - nprev + j + tb, :]
    bo_ref[...] = xp_ref[base + t_valid - nprev:base + t_valid, :]
    xp_ref[base - nprev:base, :] = xp_ref[base + tb - nprev:base + tb, :]

    xb = xc.astype(BF16)
    r = _sigmoid(jnp.dot(xb, wa_ref[...], preferred_element_type=F32) + ba_ref[...])
    ig = _sigmoid(jnp.dot(xb, wx_ref[...], preferred_element_type=F32) + bx_ref[...])
    z = -lam_ref[...]
    softplus = jnp.maximum(z, 0.0) + jnp.log(1.0 + jnp.exp(-jnp.abs(z)))
    log_a = -LRU_C * r * softplus
    a = jnp.exp(log_a)
    u = jnp.sqrt(jnp.maximum(1.0 - a * a, SQRT_EPS)) * ig * xc
    first = lax.broadcasted_iota(jnp.int32, (tb, LRU_W), 0) == 0
    u = u + jnp.where(first, a * hc_ref[...], 0.0)

    a_ref[0:pad, :] = jnp.ones((pad, LRU_W), F32)
    u_ref[0:pad, :] = jnp.zeros((pad, LRU_W), F32)
    a_ref[pad:pad + tb, :] = a
    u_ref[pad:pad + tb, :] = u
    d = 1
    while d < tb:
        a_cur = a_ref[pad:pad + tb, :]
        u_cur = u_ref[pad:pad + tb, :]
        a_sh = a_ref[pad - d:pad - d + tb, :]
        u_sh = u_ref[pad - d:pad - d + tb, :]
        u_ref[pad:pad + tb, :] = a_cur * u_sh + u_cur
        a_ref[pad:pad + tb, :] = a_cur * a_sh
        d *= 2
    h = u_ref[pad:pad + tb, :]
    y_ref[...] = (h * _gelu(g_ref[...].astype(F32))).astype(y_ref.dtype)
    hc_ref[...] = h[tb - 1:tb, :]
    hl_ref[...] = h[t_valid - 1:t_valid, :]


def _lru_call(hf3, hb3, w, h0, buf0, tb, t_valid):
    b, s, _ = hf3.shape
    n_t = s // tb
    pad = max(SUBLANES, tb // 2)
    cw, cb, wa_bd, ba, wx_bd, bx, lam = w
    col = lambda cidx: pl.BlockSpec((None, tb, COLW), lambda bi, ti, cidx=cidx: (bi, ti, cidx))
    full2 = lambda shape: pl.BlockSpec(shape, lambda bi, ti: (0, 0))
    vec = full2((1, LRU_W))
    per_b = lambda r: pl.BlockSpec((None, r, LRU_W), lambda bi, ti: (bi, 0, 0))
    return pl.pallas_call(
        functools.partial(_lru_kernel, tb=tb, t_valid=t_valid, pad=pad, n_t=n_t),
        grid=(b, n_t),
        in_specs=[col(C_XL), col(C_GL), full2((CONV_W, LRU_W)), vec, full2((LRU_W, LRU_W)), vec,
                  full2((LRU_W, LRU_W)), vec, vec, per_b(1), per_b(CONV_W - 1)],
        out_specs=[pl.BlockSpec((None, tb, LRU_W), lambda bi, ti: (bi, ti, 0)), per_b(1), per_b(CONV_W - 1)],
        out_shape=[jax.ShapeDtypeStruct((b, s, LRU_W), BF16),
                   jax.ShapeDtypeStruct((b, 1, LRU_W), F32),
                   jax.ShapeDtypeStruct((b, CONV_W - 1, LRU_W), F32)],
        scratch_shapes=[pltpu.VMEM((SUBLANES + tb, LRU_W), F32),
                        pltpu.VMEM((pad + tb, LRU_W), F32),
                        pltpu.VMEM((pad + tb, LRU_W), F32),
                        pltpu.VMEM((1, LRU_W), F32)],
        compiler_params=_params(2),
        name="rglru",
    )(hf3, hb3, cw, cb, wa_bd, ba, wx_bd, bx, lam, h0, buf0)


def _block_diag(w):
    eye = jnp.eye(LRU_BLOCKS, dtype=w.dtype).reshape(LRU_BLOCKS, 1, LRU_BLOCKS, 1)
    return (w[:, :, None, :] * eye).reshape(LRU_W, LRU_W)


def _merge_kernel(x_ref, oa_ref, oh_ref, ol_ref, g0a, g0b, g1a, g1b, g2a, g2b,
                  wa_ref, wh_ref, wl_ref, wo_ref, lg_ref, lb_ref, y_ref, *, alpha):
    def branch(o_ref, w_ref, ga, gb):
        y = jnp.dot(o_ref[...].astype(BF16), w_ref[...], preferred_element_type=F32)
        gate = jnp.concatenate([ga[...], gb[...]], axis=1).astype(F32)
        return _sigmoid(gate) * y

    merged = branch(oa_ref, wa_ref, g0a, g0b) + branch(oh_ref, wh_ref, g1a, g1b) + branch(ol_ref, wl_ref, g2a, g2b)
    mix = jnp.dot(merged.astype(BF16), wo_ref[...], preferred_element_type=F32)
    y_ref[...] = _layer_norm(alpha * x_ref[...] + mix, lg_ref[...], lb_ref[...])


def _merge_call(x2, hb2, o_att, o_hg, o_lru, w, layer, tm, alpha):
    m = x2.shape[0]
    w_att, w_hg, w_lru, w_o, ln_g, ln_b = w
    row = lambda width: pl.BlockSpec((tm, width), lambda i: (i, 0))
    col = lambda cidx: pl.BlockSpec((tm, COLW), lambda i, cidx=cidx: (i, cidx))
    full = lambda shape: pl.BlockSpec(shape, lambda i: (0, 0))
    stacked = lambda shape: pl.BlockSpec((None,) + shape, lambda i: (layer, 0, 0))
    return pl.pallas_call(
        functools.partial(_merge_kernel, alpha=alpha),
        grid=(m // tm,),
        in_specs=[row(D_MODEL), row(ATT_W), row(HG_W), row(LRU_W)]
                 + [col(C_GM + j) for j in range(6)]
                 + [stacked((ATT_W, D_MODEL)), stacked((HG_W, D_MODEL)), stacked((LRU_W, D_MODEL)),
                    stacked((D_MODEL, D_MODEL)), full((1, D_MODEL)), full((1, D_MODEL))],
        out_specs=row(D_MODEL),
        out_shape=jax.ShapeDtypeStruct((m, D_MODEL), F32),
        compiler_params=_params(1),
        name="merge_ln",
    )(x2, o_att, o_hg, o_lru, hb2, hb2, hb2, hb2, hb2, hb2, w_att, w_hg, w_lru, w_o, ln_g, ln_b)


def _ffn_kernel(x_ref, wup_ref, cw_ref, cb_ref, wdn_ref, lg_ref, lb_ref, p2_ref, p1_ref,
                y_ref, tail_ref, carry_ref, *, tm, ts, fc, alpha, per_row_prev, n_t):
    i = pl.program_id(0)
    x = x_ref[...]
    xb = x.astype(BF16)
    tpos = lax.broadcasted_iota(jnp.int32, (tm, fc), 0) % ts
    cw = cw_ref[...]
    cb = cb_ref[...]

    if not per_row_prev:
        @pl.when(i % n_t == 0)
        def _():
            carry_ref[...] = p2_ref[...]

    acc = jnp.zeros((tm, D_MODEL), F32)
    for cidx in range(D_FF // fc):
        lo, hi = cidx * fc, (cidx + 1) * fc
        u = jnp.dot(xb, wup_ref[:, lo:hi], preferred_element_type=F32)
        val = jnp.dot(xb, wup_ref[:, D_FF + lo:D_FF + hi], preferred_element_type=F32)
        if per_row_prev:
            prev2 = p2_ref[:, lo:hi]
            prev1 = p1_ref[:, lo:hi]
        else:
            cm2 = carry_ref[0:1, lo:hi]
            cm1 = carry_ref[1:2, lo:hi]
            prev2 = jnp.where(tpos == 0, cm2, cm1)
            prev1 = jnp.broadcast_to(cm1, (tm, fc))
        u1 = jnp.where(tpos >= 1, pltpu.roll(u, 1, 0), prev1)
        u2 = jnp.where(tpos >= 2, pltpu.roll(u, 2, 0), prev2)
        uc = cb[:, lo:hi] + cw[0:1, lo:hi] * u2 + cw[1:2, lo:hi] * u1 + cw[2:3, lo:hi] * u
        if per_row_prev:
            tail_ref[:, lo:hi] = u
        else:
            carry_ref[:, lo:hi] = u[tm - 2:tm, :]
            tail_ref[:, lo:hi] = u[tm - SUBLANES:tm, :]
        gated = (_gelu(uc) * val).astype(BF16)
        acc = acc + jnp.dot(gated, wdn_ref[lo:hi, :], preferred_element_type=F32)
    y_ref[...] = _layer_norm(alpha * x + acc, lg_ref[...], lb_ref[...])


def _ffn_call(x2, w, layer, prev2, prev1, tm, ts, n_seq, alpha, per_row_prev):
    m = x2.shape[0]
    w_up, cw, cb, w_dn, ln_g, ln_b = w
    n_t = (m // n_seq) // tm if not per_row_prev else 1
    fc = 1024
    row = pl.BlockSpec((tm, D_MODEL), lambda i: (i, 0))
    full = lambda shape: pl.BlockSpec(shape, lambda i: (0, 0))
    resident = lambda shape: pl.BlockSpec((None,) + shape, lambda i: (layer, 0, 0), pipeline_mode=pl.Buffered(1))
    if per_row_prev:
        p2_spec = pl.BlockSpec((tm, D_FF), lambda i: (i, 0))
        p1_spec = pl.BlockSpec((tm, D_FF), lambda i: (i, 0))
        tail_spec = pl.BlockSpec((tm, D_FF), lambda i: (i, 0))
        tail_shape = jax.ShapeDtypeStruct((m, D_FF), F32)
    else:
        p2_spec = pl.BlockSpec((None, FFN_CONV_W - 1, D_FF), lambda i: (i // n_t, 0, 0))
        p1_spec = pl.BlockSpec((None, FFN_CONV_W - 1, D_FF), lambda i: (i // n_t, 0, 0))
        tail_spec = pl.BlockSpec((None, SUBLANES, D_FF), lambda i: (i // n_t, 0, 0))
        tail_shape = jax.ShapeDtypeStruct((n_seq, SUBLANES, D_FF), F32)
    return pl.pallas_call(
        functools.partial(_ffn_kernel, tm=tm, ts=ts, fc=fc, alpha=alpha, per_row_prev=per_row_prev, n_t=n_t),
        grid=(m // tm,),
        in_specs=[row, resident((D_MODEL, 2 * D_FF)), full((FFN_CONV_W, D_FF)), full((1, D_FF)),
                  resident((D_FF, D_MODEL)), full((1, D_MODEL)), full((1, D_MODEL)), p2_spec, p1_spec],
        out_specs=[row, tail_spec],
        out_shape=[jax.ShapeDtypeStruct((m, D_MODEL), F32), tail_shape],
        scratch_shapes=[pltpu.VMEM((FFN_CONV_W - 1, D_FF), F32)],
        compiler_params=_params(1),
        name="ffn_ln",
    )(x2, w_up, cw, cb, w_dn, ln_g, ln_b, prev2, prev1)


def _trunk_layer(x3, tabs, attn_fn, hg_s0, lru_h0, lru_buf0, ffn_buf0, wl, layer, cfg):
    b, s, _ = x3.shape
    m = b * s
    x2 = x3.reshape(m, D_MODEL)
    hf2, hb2 = _matmul(x2, *wl["w_in"], layer, cfg["bm"])
    hf3 = hf2.reshape(b, s, HF_W)
    hb3 = hb2.reshape(b, s, HB_W)
    rope_out = _rope_call(hf3, tabs, cfg["rope_tm"], cfg["prompt"])
    k_rot, v_f32 = rope_out[1], rope_out[2]
    o_att = attn_fn(*rope_out)
    o_hg, hg_st = _gla_call(hf3, hb3, wl["hg_lb_logits"], wl["hg_gain"], hg_s0, layer,
                            cfg["seq_tb"], cfg["gla_c"], cfg["t_valid"], cfg["gla_bb"])
    o_lru, lru_h, lru_buf = _lru_call(hf3, hb3, wl["lru"], lru_h0, lru_buf0, cfg["seq_tb"], cfg["t_valid"])
    x1 = _merge_call(x2, hb2, o_att.reshape(m, ATT_W), o_hg.reshape(m, HG_W), o_lru.reshape(m, LRU_W),
                     wl["merge"], layer, cfg["tok_tm"], cfg["alpha"])
    if cfg["prompt"]:
        x_out, tail = _ffn_call(x1, wl["ffn"], layer, ffn_buf0, ffn_buf0, cfg["tok_tm"], cfg["tok_tm"], b,
                                cfg["alpha"], False)
        ffn_buf = tail[:, SUBLANES - (FFN_CONV_W - 1):, :]
    else:
        tv = cfg["t_valid"]
        zrow = lambda n: jnp.zeros((b, n, D_FF), F32)
        prev2 = jnp.concatenate([ffn_buf0, zrow(s - 2)], axis=1).reshape(m, D_FF)
        prev1 = jnp.concatenate([ffn_buf0[:, 1:2], zrow(s - 1)], axis=1).reshape(m, D_FF)
        x_out, u_all = _ffn_call(x1, wl["ffn"], layer, prev2, prev1, m, s, b, cfg["alpha"], True)
        ffn_buf = u_all.reshape(b, s, D_FF)[:, tv - (FFN_CONV_W - 1):tv]
    return (x_out.reshape(b, s, D_MODEL), k_rot, v_f32, hg_st, lru_h.reshape(b, LRU_W), lru_buf, ffn_buf)


def kernel(x_prompt, x_sample, cache_k, cache_v, page_table, state_hgrn, state_lru_h, state_lru_conv, state_ffn_conv, w_in, hg_lb_logits, hg_gain, lru_conv_w, lru_conv_b, lru_wa, lru_ba, lru_wx, lru_bx, lru_lambda, w_br_att, w_br_hg, w_br_lru, w_o, ln1_g, ln1_b, ffn_w_up, ffn_conv_w, ffn_conv_b, ffn_w_down, ln2_g, ln2_b):
    bsz, seq, _ = x_prompt.shape
    dbs, n_tok, _ = x_sample.shape
    depth = w_in.shape[0]
    page = cache_k.shape[2]
    past = page_table.shape[1] * page
    assert seq % MOBA_BLOCK == 0 and past % MOBA_BLOCK == 0
    assert CONV_W - 1 <= n_tok <= SUBLANES
    alpha = (2.0 * depth) ** 0.25
    tpad = SUBLANES
    n_blk = seq // MOBA_BLOCK
    nbp = -(-n_blk // SUBLANES) * SUBLANES

    cache_kt = jnp.transpose(cache_k, (0, 1, 3, 4, 2))
    cache_vt = jnp.transpose(cache_v, (0, 1, 3, 4, 2))
    tabs_p = _rope_tables(jnp.arange(seq))
    tabs_s = _rope_tables(past + jnp.arange(tpad))

    seq_tb = min(512, seq)
    assert seq % seq_tb == 0 and (bsz * seq) % min(1024, bsz * seq) == 0
    cfg_p = dict(prompt=True, bm=min(1024, bsz * seq), rope_tm=min(512, seq), seq_tb=seq_tb,
                 gla_c=16, gla_bb=math.gcd(bsz, 2), t_valid=seq_tb, tok_tm=min(512, seq), alpha=alpha)
    cfg_s = dict(prompt=False, bm=dbs * tpad, rope_tm=tpad, seq_tb=tpad,
                 gla_c=tpad, gla_bb=math.gcd(dbs, 4), t_valid=n_tok, tok_tm=dbs * tpad, alpha=alpha)

    xp = x_prompt
    xs = jnp.pad(x_sample, ((0, 0), (0, tpad - n_tok), (0, 0)))
    hg0_p = jnp.zeros((bsz, HG_HEADS, HG_DK, HG_DK), F32)
    lh0_p = jnp.zeros((bsz, 1, LRU_W), F32)
    lb0_p = jnp.zeros((bsz, CONV_W - 1, LRU_W), F32)
    fb0_p = jnp.zeros((bsz, FFN_CONV_W - 1, D_FF), F32)

    def attn_p(q_rot, k_rot, v_f32, k_bf, v_t, ksum):
        ks = jnp.pad(ksum.reshape(bsz, n_blk, ATT_W), ((0, 0), (0, nbp - n_blk), (0, 0)))
        return _attn_prompt_call(q_rot, k_bf, v_t, ks)

    w_att_bf, w_hg_bf, w_lru_bf, w_o_bf, w_up_bf, w_dn_bf = (
        w.astype(BF16) for w in (w_br_att, w_br_hg, w_br_lru, w_o, ffn_w_up, ffn_w_down))
    gh0, xl0, gl0 = 3 * ATT_W + 3 * HG_W, 3 * ATT_W + 4 * HG_W, 3 * ATT_W + 4 * HG_W + LRU_W
    w_in_f = jnp.concatenate([w_in[:, :, :gh0], w_in[:, :, xl0:gl0]], axis=2).astype(BF16)
    w_in_b = jnp.concatenate([w_in[:, :, gh0:xl0], w_in[:, :, gl0:]], axis=2).astype(BF16)

    outs_p = [[] for _ in range(6)]
    outs_s = [[] for _ in range(6)]
    row = lambda a: a.reshape(1, -1)
    for l in range(depth):
        wl = dict(
            w_in=(w_in_f, w_in_b),
            hg_lb_logits=hg_lb_logits,
            hg_gain=row(hg_gain[l]),
            lru=(lru_conv_w[l], row(lru_conv_b[l]), _block_diag(lru_wa[l]).astype(BF16), row(lru_ba[l]),
                 _block_diag(lru_wx[l]).astype(BF16), row(lru_bx[l]), row(lru_lambda[l])),
            merge=(w_att_bf, w_hg_bf, w_lru_bf, w_o_bf, row(ln1_g[l]), row(ln1_b[l])),
            ffn=(w_up_bf, ffn_conv_w[l], row(ffn_conv_b[l]), w_dn_bf, row(ln2_g[l]), row(ln2_b[l])),
        )

        def attn_s(q_rot, k_rot, v_f32, l=l):
            return _attn_sample_call(page_table, q_rot, k_rot, v_f32, cache_kt, cache_vt, l, n_tok)

        xp, *new_p = _trunk_layer(xp, tabs_p, attn_p, hg0_p, lh0_p, lb0_p, fb0_p, wl, l, cfg_p)
        xs, *new_s = _trunk_layer(xs, tabs_s, attn_s, _state_t(state_hgrn[:, l]),
                                  state_lru_h[:, l].reshape(dbs, 1, LRU_W), state_lru_conv[:, l],
                                  state_ffn_conv[:, l], wl, l, cfg_s)
        for lst, a in zip(outs_p, new_p):
            lst.append(a)
        for lst, a in zip(outs_s, new_s):
            lst.append(a)

    def heads(a, t):
        return a.reshape(a.shape[0], a.shape[1], t, ATT_HEADS, HEAD_DIM)

    def heads_t(a):
        return jnp.transpose(a.reshape(bsz, depth, ATT_HEADS, HEAD_DIM, seq), (0, 1, 4, 2, 3))

    k_p = heads_t(jnp.stack(outs_p[0], axis=1))
    v_p = heads_t(jnp.stack(outs_p[1], axis=1))
    k_s = heads(jnp.stack([a[:, :n_tok] for a in outs_s[0]], axis=1), n_tok)
    v_s = heads(jnp.stack([a[:, :n_tok] for a in outs_s[1]], axis=1), n_tok)
    hg_p = _state_t(jnp.stack(outs_p[2], axis=1))
    hg_s = _state_t(jnp.stack(outs_s[2], axis=1))
    lh_p = jnp.stack(outs_p[3], axis=1)
    lh_s = jnp.stack(outs_s[3], axis=1)
    lc_p = jnp.stack(outs_p[4], axis=1)
    lc_s = jnp.stack(outs_s[4], axis=1)
    fc_p = jnp.stack(outs_p[5], axis=1)
    fc_s = jnp.stack(outs_s[5], axis=1)
    return (xp, xs[:, :n_tok], k_p, v_p, k_s, v_s, hg_p, hg_s, lh_p, lh_s, lc_p, lc_s, fc_p, fc_s)
```

```python
import functools
import math

import jax
import jax.numpy as jnp
from jax import lax
from jax.experimental import pallas as pl
from jax.experimental.pallas import tpu as pltpu

F32 = jnp.float32
BF16 = jnp.bfloat16

D_MODEL = 1024
ATT_HEADS = 8
HEAD_DIM = 64
ATT_W = ATT_HEADS * HEAD_DIM
MOBA_BLOCK = 256
MOBA_TOPK = 3
ROPE_THETA = 500000.0
ROPE_DIM = HEAD_DIM // 4
HG_HEADS = 8
HG_DK = 64
HG_W = HG_HEADS * HG_DK
LRU_W = 512
LRU_BLOCKS = 8
LRU_BD = LRU_W // LRU_BLOCKS
LRU_C = 8.0
CONV_W = 4
D_FF = 3 * D_MODEL
FFN_CONV_W = 3
N_BRANCH = 3
LN_EPS = 1e-5
RMS_EPS = 1e-6
SQRT_EPS = 1e-12
NEG_INF = -1e30
D_IN = 3 * ATT_W + 4 * HG_W + 2 * LRU_W + N_BRANCH * D_MODEL
LOG2E = math.log2(math.e)
ATTN_GROUP = 2
ATTN_TILE_BLOCKS = 2
SAMPLE_BLOCKS_PER_STEP = 16

COLW = 512
C_QA, C_KA, C_VA, C_QH, C_FH, C_IH, C_XL = 0, 1, 2, 3, 4, 5, 6
C_GH, C_GL, C_GM = 0, 1, 2
HF_W = 7 * COLW
HB_W = 2 * COLW + N_BRANCH * D_MODEL

LANES = 128
SUBLANES = 8
MXU_DIM = 256
VMEM_LIMIT = 56 * 1024 * 1024

HIGHEST = lax.Precision.HIGHEST
CONTRACT_LAST = (((1,), (1,)), ((), ()))
CONTRACT_FIRST = (((0,), (0,)), ((), ()))


def _params(n_grid):
    return pltpu.CompilerParams(dimension_semantics=("arbitrary",) * n_grid,
                                vmem_limit_bytes=VMEM_LIMIT)


def _sigmoid(x):
    return 1.0 / (1.0 + jnp.exp(-x))


def _gelu(x):
    c = math.sqrt(2.0 / math.pi)
    return x * (0.5 * (1.0 + jnp.tanh(c * (x + 0.044715 * (x * x * x)))))


def _layer_norm(y, g, b):
    mu = jnp.mean(y, axis=-1, keepdims=True)
    d = y - mu
    var = jnp.mean(d * d, axis=-1, keepdims=True)
    return d * lax.rsqrt(var + LN_EPS) * g + b


def _split_dot(x, w01, parts):
    w = w01.astype(BF16)
    acc = None
    rem = x
    for _ in range(parts):
        piece = rem.astype(BF16)
        rem = rem - piece.astype(F32)
        y = jnp.dot(piece, w, preferred_element_type=F32)
        acc = y if acc is None else acc + y
    return acc


def _top_k_mask(gate, valid, axis):
    idx_all = lax.broadcasted_iota(jnp.int32, gate.shape, axis).astype(F32)
    g = jnp.where(valid, gate, -jnp.inf)
    sel = jnp.zeros(gate.shape, F32)
    for _ in range(MOBA_TOPK):
        mx = jnp.max(g, axis=axis, keepdims=True)
        idx = jnp.min(jnp.where(g == mx, idx_all, float(gate.shape[axis])), axis=axis, keepdims=True)
        pick = idx_all == idx
        sel = jnp.where(pick, 1.0, sel)
        g = jnp.where(pick, -jnp.inf, g)
    return jnp.where(valid, sel, 0.0)


def _mm_kernel(x_ref, wf_ref, wb_ref, of_ref, ob_ref, *, nf):
    j = pl.program_id(1)
    xb = x_ref[...].astype(BF16)

    @pl.when(j < nf)
    def _():
        of_ref[...] = jnp.dot(xb, wf_ref[...], preferred_element_type=F32)

    @pl.when(j >= nf)
    def _():
        ob_ref[...] = jnp.dot(xb, wb_ref[...], preferred_element_type=F32).astype(BF16)


def _matmul(x, wf, wb, layer, bm):
    m, k = x.shape
    bnf, bnb = HF_W // 2, HB_W // 2
    nf, nb = HF_W // bnf, HB_W // bnb
    assert m % bm == 0
    jf = lambda j: jnp.minimum(j, nf - 1)
    jb = lambda j: jnp.maximum(j - nf, 0)
    return pl.pallas_call(
        functools.partial(_mm_kernel, nf=nf),
        grid=(m // bm, nf + nb),
        in_specs=[pl.BlockSpec((bm, k), lambda i, j: (i, 0)),
                  pl.BlockSpec((None, k, bnf), lambda i, j: (layer, 0, jf(j))),
                  pl.BlockSpec((None, k, bnb), lambda i, j: (layer, 0, jb(j)))],
        out_specs=[pl.BlockSpec((bm, bnf), lambda i, j: (i, jf(j))),
                   pl.BlockSpec((bm, bnb), lambda i, j: (i, jb(j)))],
        out_shape=[jax.ShapeDtypeStruct((m, HF_W), F32), jax.ShapeDtypeStruct((m, HB_W), BF16)],
        compiler_params=_params(2),
        name="in_proj",
    )(x, wf, wb)


def _rope_kernel(q_ref, k_ref, v_ref, c_ref, sa_ref, sb_ref, qo_ref, ko_ref, vo_ref, *extra, n_blk):
    c = c_ref[...]
    sa = sa_ref[...]
    sb = sb_ref[...]

    def rot(x):
        outs = []
        for j in range(ATT_W // LANES):
            xs = x[:, LANES * j:LANES * (j + 1)]
            outs.append(xs * c + pltpu.roll(xs, ROPE_DIM // 2, 1) * sa
                        + pltpu.roll(xs, LANES - ROPE_DIM // 2, 1) * sb)
        return jnp.concatenate(outs, axis=1)

    q = rot(q_ref[...])
    k = rot(k_ref[...])
    v = v_ref[...]
    qo_ref[...] = q
    if n_blk:
        kb_ref, vt_ref, ks_ref = extra
        kb_ref[...] = k.astype(BF16)
        v_t = v.T
        ko_ref[...] = k.T
        vo_ref[...] = v_t
        for r in range(n_blk):
            rows = slice(MOBA_BLOCK * r, MOBA_BLOCK * (r + 1))
            ks_ref[r] = jnp.sum(k[rows], axis=0, keepdims=True)
            vt_ref[r] = v_t[:, rows].astype(BF16)
    else:
        ko_ref[...] = k
        vo_ref[...] = v


def _rope_call(h3, tabs, tm, prompt):
    b, s, _ = h3.shape
    nt = s // tm
    n_blk = tm // MOBA_BLOCK if prompt else 0
    col = lambda cidx: pl.BlockSpec((None, tm, COLW), lambda bi, ti, cidx=cidx: (bi, ti, cidx))
    tab = pl.BlockSpec((tm, LANES), lambda bi, ti: (ti, 0))
    row = pl.BlockSpec((None, tm, ATT_W), lambda bi, ti: (bi, ti, 0))
    out_specs = [row, row, row]
    out_shape = [jax.ShapeDtypeStruct((b, s, ATT_W), F32)] * 3
    if prompt:
        row_t = pl.BlockSpec((None, ATT_W, tm), lambda bi, ti: (bi, 0, ti))
        out_specs = [row, row_t, row_t]
        out_shape = [jax.ShapeDtypeStruct((b, s, ATT_W), F32)] + [jax.ShapeDtypeStruct((b, ATT_W, s), F32)] * 2
        out_specs += [row,
                      pl.BlockSpec((None, n_blk, ATT_W, MOBA_BLOCK), lambda bi, ti: (bi, ti, 0, 0)),
                      pl.BlockSpec((None, n_blk, 1, ATT_W), lambda bi, ti: (bi, ti, 0, 0))]
        out_shape += [jax.ShapeDtypeStruct((b, s, ATT_W), BF16),
                      jax.ShapeDtypeStruct((b, s // MOBA_BLOCK, ATT_W, MOBA_BLOCK), BF16),
                      jax.ShapeDtypeStruct((b, s // MOBA_BLOCK, 1, ATT_W), F32)]
    return pl.pallas_call(
        functools.partial(_rope_kernel, n_blk=n_blk),
        grid=(b, nt),
        in_specs=[col(C_QA), col(C_KA), col(C_VA), tab, tab, tab],
        out_specs=out_specs,
        out_shape=out_shape,
        compiler_params=_params(2),
        name="rope",
    )(h3, h3, h3, *tabs)


def _rope_tables(pos):
    half = ROPE_DIM // 2
    inv = ROPE_THETA ** (-jnp.arange(half, dtype=F32) / half)
    ang = pos.astype(F32)[:, None] * inv[None, :]
    cos, sin = jnp.cos(ang), jnp.sin(ang)
    lh = jnp.arange(LANES) % HEAD_DIM
    fi = lh % half
    c = jnp.where(lh[None, :] < ROPE_DIM, cos[:, fi], 1.0)
    sa = jnp.where((lh[None, :] >= half) & (lh[None, :] < ROPE_DIM), sin[:, fi], 0.0)
    sb = jnp.where(lh[None, :] < half, -sin[:, fi], 0.0)
    return c.astype(F32), sa.astype(F32), sb.astype(F32)


def _attn_prompt_kernel(q_ref, k_ref, vt_ref, ks_ref, o_ref, bias_ref, sa_ref, sb_ref, *, tk, qb, nbp, nb, n_blk):
    i = pl.program_id(2)
    tq = qb * tk
    first_blk = qb * i
    qscale = HEAD_DIM ** -0.5 * LOG2E
    q = q_ref[...]
    kmean = ks_ref[...] * (1.0 / MOBA_BLOCK)
    lane = lax.broadcasted_iota(jnp.int32, (tq, LANES), 1)
    valid = (lax.broadcasted_iota(jnp.int32, (nbp, tq), 0)
             < first_blk + lax.broadcasted_iota(jnp.int32, (nbp, tq), 1) // tk)
    key_row = lax.broadcasted_iota(jnp.int32, (tk, tq), 0)
    col = lax.broadcasted_iota(jnp.int32, (tk, tq), 1)
    col_blk = col // tk

    qs = []
    for hh in range(2):
        in_head = (lane >= hh * HEAD_DIM) & (lane < (hh + 1) * HEAD_DIM)
        qh = jnp.where(in_head, q, 0.0)
        gate_t = lax.dot_general(kmean, qh, CONTRACT_LAST, precision=HIGHEST, preferred_element_type=F32)
        sel = _top_k_mask(gate_t, valid, 0)
        bias_ref[hh, 0:nbp, :] = jnp.where(sel > 0.0, 0.0, NEG_INF)
        bias_ref[hh, nbp:nbp + SUBLANES, :] = jnp.full((SUBLANES, tq), NEG_INF, F32)
        qs.append((qh * qscale).astype(BF16))

    state = []
    for hh in range(2):
        ss = []
        for d in range(qb):
            blk = first_blk + d
            kd = k_ref[pl.ds(pl.multiple_of(blk * tk, tk), tk), :]
            s = lax.dot_general(kd, qs[hh], CONTRACT_LAST, preferred_element_type=F32)
            own = jnp.where(key_row <= col - d * tk, s, NEG_INF)
            if d + 1 < qb:
                later = s + bias_ref[hh, pl.ds(blk, 1), :]
                s = jnp.where(col_blk == d, own, jnp.where(col_blk > d, later, NEG_INF))
            else:
                s = jnp.where(col_blk == d, own, NEG_INF)
            ss.append(s)
        m = functools.reduce(jnp.maximum, [jnp.max(s, axis=0, keepdims=True) for s in ss])
        l = jnp.zeros((1, tq), F32)
        acc = jnp.zeros((HEAD_DIM, tq), F32)
        for d in range(qb):
            p = jnp.exp2(ss[d] - m)
            l = l + jnp.sum(p, axis=0, keepdims=True)
            acc = acc + jnp.dot(vt_ref[first_blk + d][hh * HEAD_DIM:(hh + 1) * HEAD_DIM, :], p.astype(BF16),
                                preferred_element_type=F32)
        state += [m, l, acc]

    def form_scores(jg, s_ref):
        gmax = [None, None]
        for r in range(nb):
            j = jg * nb + r
            jk = jnp.minimum(j, n_blk - 1)
            jb = jnp.where(j < first_blk, j, nbp)
            kj = k_ref[pl.ds(pl.multiple_of(jk * tk, tk), tk), :]
            for hh in range(2):
                s = (lax.dot_general(kj, qs[hh], CONTRACT_LAST, preferred_element_type=F32)
                     + bias_ref[hh, pl.ds(jb, 1), :])
                s_ref[r * 2 + hh] = s
                mx = jnp.max(s, axis=0, keepdims=True)
                gmax[hh] = mx if gmax[hh] is None else jnp.maximum(gmax[hh], mx)
        return gmax

    def softmax_pv(jg, s_ref, gmax, st):
        out = []
        for hh in range(2):
            m, l, acc = st[3 * hh:3 * hh + 3]
            m_new = jnp.maximum(m, gmax[hh])
            alpha = jnp.exp2(m - m_new)
            l = alpha * l
            acc = alpha * acc
            for r in range(nb):
                vtj = vt_ref[jnp.minimum(jg * nb + r, n_blk - 1)]
                p = jnp.exp2(s_ref[r * 2 + hh] - m_new)
                l = l + jnp.sum(p, axis=0, keepdims=True)
                acc = acc + jnp.dot(vtj[hh * HEAD_DIM:(hh + 1) * HEAD_DIM, :], p.astype(BF16),
                                    preferred_element_type=F32)
            out += [m_new, l, acc]
        return out

    def body(it, carry):
        st, gmax_a = list(carry[:6]), list(carry[6:])
        gmax_b = form_scores(2 * it + 1, sb_ref)
        st = softmax_pv(2 * it, sa_ref, gmax_a, st)
        gmax_a = form_scores(2 * it + 2, sa_ref)
        st = softmax_pv(2 * it + 1, sb_ref, gmax_b, st)
        return tuple(st) + tuple(gmax_a)

    n_groups = (first_blk + nb - 1) // nb
    res = lax.fori_loop(0, (n_groups + 1) // 2, body, tuple(state) + tuple(form_scores(0, sa_ref)))
    st = res[:6]
    o_t = jnp.concatenate([st[2] / st[1], st[5] / st[4]], axis=0)
    o_ref[...] = o_t.T.astype(o_ref.dtype)


def _attn_prompt_call(q, kb, vt, ksum):
    b, s, _ = q.shape
    tk = MOBA_BLOCK
    n_blk = s // tk
    qb = math.gcd(n_blk, ATTN_TILE_BLOCKS)
    tq = qb * tk
    nbp = ksum.shape[1]
    nb = math.gcd(n_blk, ATTN_GROUP)
    n_pairs = ATT_W // LANES
    return pl.pallas_call(
        functools.partial(_attn_prompt_kernel, tk=tk, qb=qb, nbp=nbp, nb=nb, n_blk=n_blk),
        grid=(b, n_pairs, n_blk // qb),
        in_specs=[pl.BlockSpec((None, tq, LANES), lambda bi, hp, i: (bi, i, hp)),
                  pl.BlockSpec((None, s, LANES), lambda bi, hp, i: (bi, 0, hp)),
                  pl.BlockSpec((None, n_blk, LANES, tk), lambda bi, hp, i: (bi, 0, hp, 0)),
                  pl.BlockSpec((None, nbp, LANES), lambda bi, hp, i: (bi, 0, hp))],
        out_specs=pl.BlockSpec((None, tq, LANES), lambda bi, hp, i: (bi, i, hp)),
        out_shape=jax.ShapeDtypeStruct((b, s, ATT_W), BF16),
        scratch_shapes=[pltpu.VMEM((2, nbp + SUBLANES, tq), F32),
                        pltpu.VMEM((2 * nb, tk, tq), F32),
                        pltpu.VMEM((2 * nb, tk, tq), F32)],
        compiler_params=_params(3),
        name="attn_prompt",
    )(q, kb, vt, ksum)


def _attn_sample_kernel(pt_ref, q_ref, kn_ref, vn_ref, *refs, n_tok, n_blocks, tpad, page, bps):
    del pt_ref
    kp = refs[:2 * bps]
    vp = refs[2 * bps:4 * bps]
    o_ref, m_ref, l_ref, g_ref, oall_ref = refs[4 * bps:]
    n = pl.program_id(1)
    n_steps = n_blocks // bps
    rows = n_tok * ATT_HEADS
    scale = HEAD_DIM ** -0.5
    q = q_ref[...]
    qrep = jnp.concatenate([jnp.broadcast_to(q[t:t + 1], (ATT_HEADS, ATT_W)) for t in range(n_tok)], axis=0)
    head_mask = (lax.broadcasted_iota(jnp.int32, (rows, ATT_W), 0) % ATT_HEADS
                 == lax.broadcasted_iota(jnp.int32, (rows, ATT_W), 1) // HEAD_DIM)
    qs = (jnp.where(head_mask, qrep, 0.0) * scale).astype(BF16)
    lane = lax.broadcasted_iota(jnp.int32, (rows, LANES), 1)

    @pl.when(n == 0)
    def _():
        m_ref[...] = jnp.full(m_ref.shape, NEG_INF, F32)
        l_ref[...] = jnp.zeros(l_ref.shape, F32)
        g_ref[...] = jnp.zeros(g_ref.shape, F32)

    for r in range(bps):
        blk = n * bps + r
        k_t = jnp.concatenate([kp[2 * r][...].reshape(ATT_W, page), kp[2 * r + 1][...].reshape(ATT_W, page)],
                              axis=1).astype(BF16)
        v_t = jnp.concatenate([vp[2 * r][...].reshape(ATT_W, page), vp[2 * r + 1][...].reshape(ATT_W, page)],
                              axis=1).astype(BF16)
        s = jnp.dot(qs, k_t, preferred_element_type=F32)
        g_n = jnp.sum(s, axis=1, keepdims=True) * (1.0 / (scale * MOBA_BLOCK))
        m_n = jnp.max(s, axis=1, keepdims=True)
        p = jnp.exp(s - m_n)
        l_n = jnp.sum(p, axis=1, keepdims=True)
        oall_ref[blk] = lax.dot_general(p.astype(BF16), v_t, CONTRACT_LAST, preferred_element_type=F32)
        m_ref[...] = jnp.where(lane == blk, m_n, m_ref[...])
        l_ref[...] = jnp.where(lane == blk, l_n, l_ref[...])
        g_ref[...] = jnp.where(lane == blk, g_n, g_ref[...])

    @pl.when(n == n_steps - 1)
    def _():
        valid = lane < n_blocks
        sel = _top_k_mask(g_ref[...], valid, 1) > 0.0
        m_all = m_ref[...]
        l_all = l_ref[...]
        kn = kn_ref[...].astype(BF16)
        vn = vn_ref[...].astype(BF16)
        s_own = lax.dot_general(qs, kn, CONTRACT_LAST, preferred_element_type=F32)
        tq = lax.broadcasted_iota(jnp.int32, (rows, tpad), 0) // ATT_HEADS
        tk = lax.broadcasted_iota(jnp.int32, (rows, tpad), 1)
        s_own = jnp.where((tk <= tq) & (tk < n_tok), s_own, NEG_INF)
        m_own = jnp.max(s_own, axis=1, keepdims=True)
        m_fin = jnp.maximum(m_own, jnp.max(jnp.where(sel, m_all, NEG_INF), axis=1, keepdims=True))
        p_own = jnp.exp(s_own - m_fin)
        w = jnp.where(sel, jnp.exp(m_all - m_fin), 0.0)
        l_fin = jnp.sum(p_own, axis=1, keepdims=True) + jnp.sum(w * l_all, axis=1, keepdims=True)
        o = jnp.dot(p_own.astype(BF16), vn, preferred_element_type=F32)
        for nb in range(n_blocks):
            o = o + w[:, nb:nb + 1] * oall_ref[nb]
        o = jnp.where(head_mask, o / l_fin, 0.0)
        o_tok = jnp.sum(o.reshape(n_tok, ATT_HEADS, ATT_W), axis=1)
        o_ref[...] = jnp.concatenate([o_tok, jnp.zeros((tpad - n_tok, ATT_W), F32)], axis=0).astype(o_ref.dtype)


def _attn_sample_call(page_table, q, k_new, v_new, cache_kt, cache_vt, layer, n_tok):
    db, tpad, _ = q.shape
    n_pages = page_table.shape[1]
    page = cache_kt.shape[-1]
    assert MOBA_BLOCK == 2 * page, "two cache pages per MoBA block expected"
    n_blocks = n_pages // 2
    assert n_blocks <= LANES
    bps = math.gcd(n_blocks, SAMPLE_BLOCKS_PER_STEP)
    rows = n_tok * ATT_HEADS
    cpage = lambda off: pl.BlockSpec((None, None, ATT_HEADS, HEAD_DIM, page),
                                     lambda bi, n, pt, off=off: (pt[bi, 2 * bps * n + off], layer, 0, 0, 0))
    pages = [cpage(off) for off in range(2 * bps)]
    tok = pl.BlockSpec((None, tpad, ATT_W), lambda bi, n, pt: (bi, 0, 0))
    grid_spec = pltpu.PrefetchScalarGridSpec(
        num_scalar_prefetch=1,
        grid=(db, n_blocks // bps),
        in_specs=[tok, tok, tok] + pages + pages,
        out_specs=tok,
        scratch_shapes=[pltpu.VMEM((rows, LANES), F32),
                        pltpu.VMEM((rows, LANES), F32),
                        pltpu.VMEM((rows, LANES), F32),
                        pltpu.VMEM((n_blocks, rows, ATT_W), F32)],
    )
    return pl.pallas_call(
        functools.partial(_attn_sample_kernel, n_tok=n_tok, n_blocks=n_blocks, tpad=tpad, page=page, bps=bps),
        grid_spec=grid_spec,
        out_shape=jax.ShapeDtypeStruct((db, tpad, ATT_W), BF16),
        compiler_params=_params(2),
        name="attn_sample",
    )(page_table, q, k_new, v_new, *([cache_kt] * (2 * bps)), *([cache_vt] * (2 * bps)))


def _gla_kernel(q_ref, f_ref, i_ref, g_ref, lbl_ref, gain_ref, s0_ref, o_ref, so_ref,
                st_ref, lc_ref, kk_ref, acc_ref, *, bb, tb, c, **kw):
    seqs = [_gla_sequence(q_ref.at[b], f_ref.at[b], i_ref.at[b], g_ref.at[b], lbl_ref, gain_ref, s0_ref.at[b],
                          o_ref.at[b], so_ref.at[b], st_ref.at[b], lc_ref.at[b], kk_ref.at[b], acc_ref.at[b],
                          tb=tb, c=c, **kw)
            for b in range(bb)]

    def chunk(cidx, carry):
        for chunk_step, _ in seqs:
            chunk_step(cidx)
        return carry

    lax.fori_loop(0, tb // c, chunk, 0)
    for _, finish in seqs:
        finish()


def _gla_sequence(q_ref, f_ref, i_ref, g_ref, lbl_ref, gain_ref, s0_ref, o_ref, so_ref,
                  st_ref, lc_ref, kk_ref, acc_ref, *, layer, tb, c, t_valid, n_t):
    t = pl.program_id(1)
    half = MXU_DIM

    @pl.when(t == 0)
    def _():
        st_ref[...] = jnp.zeros(st_ref.shape, F32)
        for h in range(HG_HEADS):
            d0 = (h % 4) * HG_DK
            st_ref[h // 4, d0:d0 + HG_DK, d0:d0 + HG_DK] = s0_ref[h]

    lg = lbl_ref[...]
    e = jnp.exp(lg - jnp.max(lg, axis=0, keepdims=True))
    den = jnp.sum(e, axis=0, keepdims=True)
    lb = jnp.zeros((1, HG_W), F32)
    for j in range(1, layer + 1):
        lb = lb + e[j:j + 1] / den

    fr = f_ref[...]
    f = lb + (1.0 - lb) * _sigmoid(fr)
    logf2 = jnp.log2(f)
    kk = (1.0 - lb) * _sigmoid(-fr)
    if t_valid < tb:
        live = lax.broadcasted_iota(jnp.int32, (tb, HG_W), 0) < t_valid
        logf2 = jnp.where(live, logf2, 0.0)
        kk = jnp.where(live, kk, 0.0)
    kk_ref[...] = kk

    grp = min(LANES, tb)
    ri = lax.broadcasted_iota(jnp.int32, (grp, grp), 0)
    ci = lax.broadcasted_iota(jnp.int32, (grp, grp), 1)
    tri = jnp.where((ri // c == ci // c) & (ci <= ri), 1.0, 0.0).astype(BF16)
    for r in range(tb // grp):
        rem = logf2[r * grp:(r + 1) * grp]
        acc = None
        for _ in range(3):
            piece = rem.astype(BF16)
            rem = rem - piece.astype(F32)
            y = jnp.dot(tri, piece, preferred_element_type=F32)
            acc = y if acc is None else acc + y
        lc_ref[r * grp:(r + 1) * grp, :] = acc

    bi = lax.broadcasted_iota(jnp.int32, (half, half), 0) // HG_DK
    bj = lax.broadcasted_iota(jnp.int32, (half, half), 1) // HG_DK
    bd = jnp.where(bi == bj, 1.0, 0.0)
    bd16 = bd.astype(BF16)
    head_lanes = lax.broadcasted_iota(jnp.int32, (c, half), 1) // HG_DK

    def chunk_step(cidx):
        r0 = pl.multiple_of(cidx * c, c)
        qc = q_ref[pl.ds(r0, c), :]
        vc = i_ref[pl.ds(r0, c), :]
        kc = kk_ref[pl.ds(r0, c), :]
        lc = lc_ref[pl.ds(r0, c), :]
        last = lc[c - 1:c, :]
        qt = (qc * jnp.exp2(lc)).astype(BF16)
        o = jnp.concatenate(
            [lax.dot_general(qt[:, g * half:(g + 1) * half], st_ref[g].astype(BF16), CONTRACT_LAST,
                             preferred_element_type=F32) for g in range(2)], axis=1)
        ws = []
        for s in range(c):
            t0 = (s // SUBLANES) * SUBLANES
            dec = jnp.exp2(lc[t0:] - lc[s:s + 1, :])
            if t0:
                ws.append(jnp.zeros((t0, HG_W), F32))
            live = lax.broadcasted_iota(jnp.int32, (c - t0, HG_W), 0) >= s - t0
            ws.append(jnp.where(live, qc[t0:] * kc[s:s + 1, :] * dec, 0.0))
        w = jnp.concatenate(ws, axis=0).astype(BF16)
        att = jnp.concatenate(
            [jnp.dot(w[:, g * half:(g + 1) * half], bd16, preferred_element_type=F32) for g in range(2)], axis=1)
        tiles = [o[t0:t0 + SUBLANES] for t0 in range(0, c, SUBLANES)]
        for s in range(c):
            for ti in range(s // SUBLANES, c // SUBLANES):
                off = s * c + ti * SUBLANES
                tiles[ti] = tiles[ti] + att[off:off + SUBLANES] * vc[s:s + 1, :]
        acc_ref[pl.ds(r0, c), :] = jnp.concatenate(tiles, axis=0)
        k2 = (kc * jnp.exp2(last - lc)).astype(BF16)
        vb = vc.astype(BF16)
        dl = jnp.exp2(last)
        for g in range(2):
            vg = vb[:, g * half:(g + 1) * half]
            kg = k2[:, g * half:(g + 1) * half]
            v_exp = jnp.concatenate([jnp.where(head_lanes == hh, vg, 0.0) for hh in range(4)], axis=0)
            k_exp = jnp.concatenate([jnp.where(head_lanes == hh, kg, 0.0) for hh in range(4)], axis=0)
            upd = lax.dot_general(v_exp, k_exp, CONTRACT_FIRST, preferred_element_type=F32)
            st_ref[g] = st_ref[g] * dl[:, g * half:(g + 1) * half] + upd

    def finish():
        o = acc_ref[...]
        o2 = o * o
        ms = jnp.concatenate([_split_dot(o2[:, g * half:(g + 1) * half], bd, 3) for g in range(2)],
                             axis=1) * (1.0 / HG_DK)
        og = g_ref[...].astype(F32)
        o_ref[...] = (o * lax.rsqrt(ms + RMS_EPS) * gain_ref[...] * (og * _sigmoid(og))).astype(o_ref.dtype)

        @pl.when(t == n_t - 1)
        def _():
            for h in range(HG_HEADS):
                d0 = (h % 4) * HG_DK
                so_ref[h] = st_ref[h // 4, d0:d0 + HG_DK, d0:d0 + HG_DK]

    return chunk_step, finish


def _gla_call(hf3, hb3, lb_logits, gain, s0_bd, layer, tb, c, t_valid, bb):
    b, s, _ = hf3.shape
    n_t = s // tb
    n_layers = lb_logits.shape[0]
    assert b % bb == 0
    col = lambda cidx: pl.BlockSpec((bb, tb, COLW), lambda bi, ti, cidx=cidx: (bi, ti, cidx))
    full2 = lambda shape: pl.BlockSpec(shape, lambda bi, ti: (0, 0))
    st_spec = pl.BlockSpec((bb, HG_HEADS, HG_DK, HG_DK), lambda bi, ti: (bi, 0, 0, 0))
    return pl.pallas_call(
        functools.partial(_gla_kernel, bb=bb, layer=layer, tb=tb, c=c, t_valid=t_valid, n_t=n_t),
        grid=(b // bb, n_t),
        in_specs=[col(C_QH), col(C_FH), col(C_IH), col(C_GH),
                  full2((n_layers, HG_W)), full2((1, HG_W)), st_spec],
        out_specs=[pl.BlockSpec((bb, tb, HG_W), lambda bi, ti: (bi, ti, 0)), st_spec],
        out_shape=[jax.ShapeDtypeStruct((b, s, HG_W), BF16),
                   jax.ShapeDtypeStruct((b, HG_HEADS, HG_DK, HG_DK), F32)],
        scratch_shapes=[pltpu.VMEM((bb, 2, MXU_DIM, MXU_DIM), F32),
                        pltpu.VMEM((bb, tb, HG_W), F32),
                        pltpu.VMEM((bb, tb, HG_W), F32),
                        pltpu.VMEM((bb, tb, HG_W), F32)],
        compiler_params=_params(2),
        name="hgrn2",
    )(hf3, hf3, hf3, hb3, lb_logits, gain, s0_bd)


def _state_t(s):
    return jnp.swapaxes(s, -1, -2)


def _lru_kernel(x_ref, g_ref, cw_ref, cb_ref, wa_ref, ba_ref, wx_ref, bx_ref, lam_ref, h0_ref, buf0_ref,
                y_ref, hl_ref, bo_ref, xp_ref, a_ref, u_ref, hc_ref, *, tb, t_valid, pad, n_t):
    t = pl.program_id(1)
    nprev = CONV_W - 1
    base = SUBLANES

    @pl.when(t == 0)
    def _():
        xp_ref[base - nprev:base, :] = buf0_ref[...]
        hc_ref[...] = h0_ref[...]

    x = x_ref[...]
    xp_ref[base:base + tb, :] = x
    cw = cw_ref[...]
    xc = cb_ref[...] + cw[nprev:nprev + 1, :] * x
    for j in range(nprev):
        xc = xc + cw[j:j + 1, :] * xp_ref[base - nprev + j:base - nprev + j + tb, :]
    bo_ref[...] = xp_ref[base + t_valid - nprev:base + t_valid, :]
    xp_ref[base - nprev:base, :] = xp_ref[base + tb - nprev:base + tb, :]

    xb = xc.astype(BF16)
    r = _sigmoid(jnp.dot(xb, wa_ref[...], preferred_element_type=F32) + ba_ref[...])
    ig = _sigmoid(jnp.dot(xb, wx_ref[...], preferred_element_type=F32) + bx_ref[...])
    z = -lam_ref[...]
    softplus = jnp.maximum(z, 0.0) + jnp.log(1.0 + jnp.exp(-jnp.abs(z)))
    log_a = -LRU_C * r * softplus
    a = jnp.exp(log_a)
    u = jnp.sqrt(jnp.maximum(1.0 - a * a, SQRT_EPS)) * ig * xc
    first = lax.broadcasted_iota(jnp.int32, (tb, LRU_W), 0) == 0
    u = u + jnp.where(first, a * hc_ref[...], 0.0)

    a_ref[0:pad, :] = jnp.ones((pad, LRU_W), F32)
    u_ref[0:pad, :] = jnp.zeros((pad, LRU_W), F32)
    a_ref[pad:pad + tb, :] = a
    u_ref[pad:pad + tb, :] = u
    d = 1
    while d < tb:
        a_cur = a_ref[pad:pad + tb, :]
        u_cur = u_ref[pad:pad + tb, :]
        a_sh = a_ref[pad - d:pad - d + tb, :]
        u_sh = u_ref[pad - d:pad - d + tb, :]
        u_ref[pad:pad + tb, :] = a_cur * u_sh + u_cur
        a_ref[pad:pad + tb, :] = a_cur * a_sh
        d *= 2
    h = u_ref[pad:pad + tb, :]
    y_ref[...] = (h * _gelu(g_ref[...].astype(F32))).astype(y_ref.dtype)
    hc_ref[...] = h[tb - 1:tb, :]
    hl_ref[...] = h[t_valid - 1:t_valid, :]


def _lru_call(hf3, hb3, w, h0, buf0, tb, t_valid):
    b, s, _ = hf3.shape
    n_t = s // tb
    pad = max(SUBLANES, tb // 2)
    cw, cb, wa_bd, ba, wx_bd, bx, lam = w
    col = lambda cidx: pl.BlockSpec((None, tb, COLW), lambda bi, ti, cidx=cidx: (bi, ti, cidx))
    full2 = lambda shape: pl.BlockSpec(shape, lambda bi, ti: (0, 0))
    vec = full2((1, LRU_W))
    per_b = lambda r: pl.BlockSpec((None, r, LRU_W), lambda bi, ti: (bi, 0, 0))
    return pl.pallas_call(
        functools.partial(_lru_kernel, tb=tb, t_valid=t_valid, pad=pad, n_t=n_t),
        grid=(b, n_t),
        in_specs=[col(C_XL), col(C_GL), full2((CONV_W, LRU_W)), vec, full2((LRU_W, LRU_W)), vec,
                  full2((LRU_W, LRU_W)), vec, vec, per_b(1), per_b(CONV_W - 1)],
        out_specs=[pl.BlockSpec((None, tb, LRU_W), lambda bi, ti: (bi, ti, 0)), per_b(1), per_b(CONV_W - 1)],
        out_shape=[jax.ShapeDtypeStruct((b, s, LRU_W), BF16),
                   jax.ShapeDtypeStruct((b, 1, LRU_W), F32),
                   jax.ShapeDtypeStruct((b, CONV_W - 1, LRU_W), F32)],
        scratch_shapes=[pltpu.VMEM((SUBLANES + tb, LRU_W), F32),
                        pltpu.VMEM((pad + tb, LRU_W), F32),
                        pltpu.VMEM((pad + tb, LRU_W), F32),
                        pltpu.VMEM((1, LRU_W), F32)],
        compiler_params=_params(2),
        name="rglru",
    )(hf3, hb3, cw, cb, wa_bd, ba, wx_bd, bx, lam, h0, buf0)


def _block_diag(w):
    eye = jnp.eye(LRU_BLOCKS, dtype=w.dtype).reshape(LRU_BLOCKS, 1, LRU_BLOCKS, 1)
    return (w[:, :, None, :] * eye).reshape(LRU_W, LRU_W)


def _merge_kernel(x_ref, oa_ref, oh_ref, ol_ref, g0a, g0b, g1a, g1b, g2a, g2b,
                  wa_ref, wh_ref, wl_ref, wo_ref, lg_ref, lb_ref, y_ref, *, alpha):
    def branch(o_ref, w_ref, ga, gb):
        y = jnp.dot(o_ref[...].astype(BF16), w_ref[...], preferred_element_type=F32)
        gate = jnp.concatenate([ga[...], gb[...]], axis=1).astype(F32)
        return _sigmoid(gate) * y

    merged = branch(oa_ref, wa_ref, g0a, g0b) + branch(oh_ref, wh_ref, g1a, g1b) + branch(ol_ref, wl_ref, g2a, g2b)
    mix = jnp.dot(merged.astype(BF16), wo_ref[...], preferred_element_type=F32)
    y_ref[...] = _layer_norm(alpha * x_ref[...] + mix, lg_ref[...], lb_ref[...])


def _merge_call(x2, hb2, o_att, o_hg, o_lru, w, layer, tm, alpha):
    m = x2.shape[0]
    w_att, w_hg, w_lru, w_o, ln_g, ln_b = w
    row = lambda width: pl.BlockSpec((tm, width), lambda i: (i, 0))
    col = lambda cidx: pl.BlockSpec((tm, COLW), lambda i, cidx=cidx: (i, cidx))
    full = lambda shape: pl.BlockSpec(shape, lambda i: (0, 0))
    stacked = lambda shape: pl.BlockSpec((None,) + shape, lambda i: (layer, 0, 0))
    return pl.pallas_call(
        functools.partial(_merge_kernel, alpha=alpha),
        grid=(m // tm,),
        in_specs=[row(D_MODEL), row(ATT_W), row(HG_W), row(LRU_W)]
                 + [col(C_GM + j) for j in range(6)]
                 + [stacked((ATT_W, D_MODEL)), stacked((HG_W, D_MODEL)), stacked((LRU_W, D_MODEL)),
                    stacked((D_MODEL, D_MODEL)), full((1, D_MODEL)), full((1, D_MODEL))],
        out_specs=row(D_MODEL),
        out_shape=jax.ShapeDtypeStruct((m, D_MODEL), F32),
        compiler_params=_params(1),
        name="merge_ln",
    )(x2, o_att, o_hg, o_lru, hb2, hb2, hb2, hb2, hb2, hb2, w_att, w_hg, w_lru, w_o, ln_g, ln_b)


def _ffn_kernel(x_ref, wup_ref, cw_ref, cb_ref, wdn_ref, lg_ref, lb_ref, p2_ref, p1_ref,
                y_ref, tail_ref, carry_ref, *, tm, ts, fc, alpha, per_row_prev, n_t):
    i = pl.program_id(0)
    x = x_ref[...]
    xb = x.astype(BF16)
    tpos = lax.broadcasted_iota(jnp.int32, (tm, fc), 0) % ts
    cw = cw_ref[...]
    cb = cb_ref[...]

    if not per_row_prev:
        @pl.when(i % n_t == 0)
        def _():
            carry_ref[...] = p2_ref[...]

    acc = jnp.zeros((tm, D_MODEL), F32)
    for cidx in range(D_FF // fc):
        lo, hi = cidx * fc, (cidx + 1) * fc
        u = jnp.dot(xb, wup_ref[:, lo:hi], preferred_element_type=F32)
        val = jnp.dot(xb, wup_ref[:, D_FF + lo:D_FF + hi], preferred_element_type=F32)
        if per_row_prev:
            prev2 = p2_ref[:, lo:hi]
            prev1 = p1_ref[:, lo:hi]
        else:
            cm2 = carry_ref[0:1, lo:hi]
            cm1 = carry_ref[1:2, lo:hi]
            prev2 = jnp.where(tpos == 0, cm2, cm1)
            prev1 = jnp.broadcast_to(cm1, (tm, fc))
        u1 = jnp.where(tpos >= 1, pltpu.roll(u, 1, 0), prev1)
        u2 = jnp.where(tpos >= 2, pltpu.roll(u, 2, 0), prev2)
        uc = cb[:, lo:hi] + cw[0:1, lo:hi] * u2 + cw[1:2, lo:hi] * u1 + cw[2:3, lo:hi] * u
        if per_row_prev:
            tail_ref[:, lo:hi] = u
        else:
            carry_ref[:, lo:hi] = u[tm - 2:tm, :]
            tail_ref[:, lo:hi] = u[tm - SUBLANES:tm, :]
        gated = (_gelu(uc) * val).astype(BF16)
        acc = acc + jnp.dot(gated, wdn_ref[lo:hi, :], preferred_element_type=F32)
    y_ref[...] = _layer_norm(alpha * x + acc, lg_ref[...], lb_ref[...])


def _ffn_call(x2, w, layer, prev2, prev1, tm, ts, n_seq, alpha, per_row_prev):
    m = x2.shape[0]
    w_up, cw, cb, w_dn, ln_g, ln_b = w
    n_t = (m // n_seq) // tm if not per_row_prev else 1
    fc = 1024
    row = pl.BlockSpec((tm, D_MODEL), lambda i: (i, 0))
    full = lambda shape: pl.BlockSpec(shape, lambda i: (0, 0))
    resident = lambda shape: pl.BlockSpec((None,) + shape, lambda i: (layer, 0, 0), pipeline_mode=pl.Buffered(1))
    if per_row_prev:
        p2_spec = pl.BlockSpec((tm, D_FF), lambda i: (i, 0))
        p1_spec = pl.BlockSpec((tm, D_FF), lambda i: (i, 0))
        tail_spec = pl.BlockSpec((tm, D_FF), lambda i: (i, 0))
        tail_shape = jax.ShapeDtypeStruct((m, D_FF), F32)
    else:
        p2_spec = pl.BlockSpec((None, FFN_CONV_W - 1, D_FF), lambda i: (i // n_t, 0, 0))
        p1_spec = pl.BlockSpec((None, FFN_CONV_W - 1, D_FF), lambda i: (i // n_t, 0, 0))
        tail_spec = pl.BlockSpec((None, SUBLANES, D_FF), lambda i: (i // n_t, 0, 0))
        tail_shape = jax.ShapeDtypeStruct((n_seq, SUBLANES, D_FF), F32)
    return pl.pallas_call(
        functools.partial(_ffn_kernel, tm=tm, ts=ts, fc=fc, alpha=alpha, per_row_prev=per_row_prev, n_t=n_t),
        grid=(m // tm,),
        in_specs=[row, resident((D_MODEL, 2 * D_FF)), full((FFN_CONV_W, D_FF)), full((1, D_FF)),
                  resident((D_FF, D_MODEL)), full((1, D_MODEL)), full((1, D_MODEL)), p2_spec, p1_spec],
        out_specs=[row, tail_spec],
        out_shape=[jax.ShapeDtypeStruct((m, D_MODEL), F32), tail_shape],
        scratch_shapes=[pltpu.VMEM((FFN_CONV_W - 1, D_FF), F32)],
        compiler_params=_params(1),
        name="ffn_ln",
    )(x2, w_up, cw, cb, w_dn, ln_g, ln_b, prev2, prev1)


def _trunk_layer(x3, tabs, attn_fn, hg_s0, lru_h0, lru_buf0, ffn_buf0, wl, layer, cfg):
    b, s, _ = x3.shape
    m = b * s
    x2 = x3.reshape(m, D_MODEL)
    hf2, hb2 = _matmul(x2, *wl["w_in"], layer, cfg["bm"])
    hf3 = hf2.reshape(b, s, HF_W)
    hb3 = hb2.reshape(b, s, HB_W)
    rope_out = _rope_call(hf3, tabs, cfg["rope_tm"], cfg["prompt"])
    k_rot, v_f32 = rope_out[1], rope_out[2]
    o_att = attn_fn(*rope_out)
    o_hg, hg_st = _gla_call(hf3, hb3, wl["hg_lb_logits"], wl["hg_gain"], hg_s0, layer,
                            cfg["seq_tb"], cfg["gla_c"], cfg["t_valid"], cfg["gla_bb"])
    o_lru, lru_h, lru_buf = _lru_call(hf3, hb3, wl["lru"], lru_h0, lru_buf0, cfg["seq_tb"], cfg["t_valid"])
    x1 = _merge_call(x2, hb2, o_att.reshape(m, ATT_W), o_hg.reshape(m, HG_W), o_lru.reshape(m, LRU_W),
                     wl["merge"], layer, cfg["tok_tm"], cfg["alpha"])
    if cfg["prompt"]:
        x_out, tail = _ffn_call(x1, wl["ffn"], layer, ffn_buf0, ffn_buf0, cfg["tok_tm"], cfg["tok_tm"], b,
                                cfg["alpha"], False)
        ffn_buf = tail[:, SUBLANES - (FFN_CONV_W - 1):, :]
    else:
        tv = cfg["t_valid"]
        zrow = lambda n: jnp.zeros((b, n, D_FF), F32)
        prev2 = jnp.concatenate([ffn_buf0, zrow(s - 2)], axis=1).reshape(m, D_FF)
        prev1 = jnp.concatenate([ffn_buf0[:, 1:2], zrow(s - 1)], axis=1).reshape(m, D_FF)
        x_out, u_all = _ffn_call(x1, wl["ffn"], layer, prev2, prev1, m, s, b, cfg["alpha"], True)
        ffn_buf = u_all.reshape(b, s, D_FF)[:, tv - (FFN_CONV_W - 1):tv]
    return (x_out.reshape(b, s, D_MODEL), k_rot, v_f32, hg_st, lru_h.reshape(b, LRU_W), lru_buf, ffn_buf)


def kernel(x_prompt, x_sample, cache_k, cache_v, page_table, state_hgrn, state_lru_h, state_lru_conv, state_ffn_conv, w_in, hg_lb_logits, hg_gain, lru_conv_w, lru_conv_b, lru_wa, lru_ba, lru_wx, lru_bx, lru_lambda, w_br_att, w_br_hg, w_br_lru, w_o, ln1_g, ln1_b, ffn_w_up, ffn_conv_w, ffn_conv_b, ffn_w_down, ln2_g, ln2_b):
    bsz, seq, _ = x_prompt.shape
    dbs, n_tok, _ = x_sample.shape
    depth = w_in.shape[0]
    page = cache_k.shape[2]
    past = page_table.shape[1] * page
    assert seq % MOBA_BLOCK == 0 and past % MOBA_BLOCK == 0
    assert CONV_W - 1 <= n_tok <= SUBLANES
    alpha = (2.0 * depth) ** 0.25
    tpad = SUBLANES
    n_blk = seq // MOBA_BLOCK
    nbp = -(-n_blk // SUBLANES) * SUBLANES

    cache_kt = jnp.transpose(cache_k, (0, 1, 3, 4, 2))
    cache_vt = jnp.transpose(cache_v, (0, 1, 3, 4, 2))
    tabs_p = _rope_tables(jnp.arange(seq))
    tabs_s = _rope_tables(past + jnp.arange(tpad))

    seq_tb = min(512, seq)
    assert seq % seq_tb == 0 and (bsz * seq) % min(1024, bsz * seq) == 0
    cfg_p = dict(prompt=True, bm=min(1024, bsz * seq), rope_tm=min(1024, seq), seq_tb=seq_tb,
                 gla_c=16, gla_bb=math.gcd(bsz, 2), t_valid=seq_tb, tok_tm=min(512, seq), alpha=alpha)
    cfg_s = dict(prompt=False, bm=dbs * tpad, rope_tm=tpad, seq_tb=tpad,
                 gla_c=tpad, gla_bb=math.gcd(dbs, 4), t_valid=n_tok, tok_tm=dbs * tpad, alpha=alpha)

    xp = x_prompt
    xs = jnp.pad(x_sample, ((0, 0), (0, tpad - n_tok), (0, 0)))
    hg0_p = jnp.zeros((bsz, HG_HEADS, HG_DK, HG_DK), F32)
    lh0_p = jnp.zeros((bsz, 1, LRU_W), F32)
    lb0_p = jnp.zeros((bsz, CONV_W - 1, LRU_W), F32)
    fb0_p = jnp.zeros((bsz, FFN_CONV_W - 1, D_FF), F32)

    def attn_p(q_rot, k_rot, v_f32, k_bf, v_t, ksum):
        ks = jnp.pad(ksum.reshape(bsz, n_blk, ATT_W), ((0, 0), (0, nbp - n_blk), (0, 0)))
        return _attn_prompt_call(q_rot, k_bf, v_t, ks)

    w_att_bf, w_hg_bf, w_lru_bf, w_o_bf, w_up_bf, w_dn_bf = (
        w.astype(BF16) for w in (w_br_att, w_br_hg, w_br_lru, w_o, ffn_w_up, ffn_w_down))
    gh0, xl0, gl0 = 3 * ATT_W + 3 * HG_W, 3 * ATT_W + 4 * HG_W, 3 * ATT_W + 4 * HG_W + LRU_W
    w_in_f = jnp.concatenate([w_in[:, :, :gh0], w_in[:, :, xl0:gl0]], axis=2).astype(BF16)
    w_in_b = jnp.concatenate([w_in[:, :, gh0:xl0], w_in[:, :, gl0:]], axis=2).astype(BF16)

    outs_p = [[] for _ in range(6)]
    outs_s = [[] for _ in range(6)]
    row = lambda a: a.reshape(1, -1)
    for l in range(depth):
        wl = dict(
            w_in=(w_in_f, w_in_b),
            hg_lb_logits=hg_lb_logits,
            hg_gain=row(hg_gain[l]),
            lru=(lru_conv_w[l], row(lru_conv_b[l]), _block_diag(lru_wa[l]).astype(BF16), row(lru_ba[l]),
                 _block_diag(lru_wx[l]).astype(BF16), row(lru_bx[l]), row(lru_lambda[l])),
            merge=(w_att_bf, w_hg_bf, w_lru_bf, w_o_bf, row(ln1_g[l]), row(ln1_b[l])),
            ffn=(w_up_bf, ffn_conv_w[l], row(ffn_conv_b[l]), w_dn_bf, row(ln2_g[l]), row(ln2_b[l])),
        )

        def attn_s(q_rot, k_rot, v_f32, l=l):
            return _attn_sample_call(page_table, q_rot, k_rot, v_f32, cache_kt, cache_vt, l, n_tok)

        xp, *new_p = _trunk_layer(xp, tabs_p, attn_p, hg0_p, lh0_p, lb0_p, fb0_p, wl, l, cfg_p)
        xs, *new_s = _trunk_layer(xs, tabs_s, attn_s, _state_t(state_hgrn[:, l]),
                                  state_lru_h[:, l].reshape(dbs, 1, LRU_W), state_lru_conv[:, l],
                                  state_ffn_conv[:, l], wl, l, cfg_s)
        for lst, a in zip(outs_p, new_p):
            lst.append(a)
        for lst, a in zip(outs_s, new_s):
            lst.append(a)

    def heads(a, t):
        return a.reshape(a.shape[0], a.shape[1], t, ATT_HEADS, HEAD_DIM)

    def heads_t(a):
        return jnp.transpose(a.reshape(bsz, depth, ATT_HEADS, HEAD_DIM, seq), (0, 1, 4, 2, 3))

    k_p = heads_t(jnp.stack(outs_p[0], axis=1))
    v_p = heads_t(jnp.stack(outs_p[1], axis=1))
    k_s = heads(jnp.stack([a[:, :n_tok] for a in outs_s[0]], axis=1), n_tok)
    v_s = heads(jnp.stack([a[:, :n_tok] for a in outs_s[1]], axis=1), n_tok)
    hg_p = _state_t(jnp.stack(outs_p[2], axis=1))
    hg_s = _state_t(jnp.stack(outs_s[2], axis=1))
    lh_p = jnp.stack(outs_p[3], axis=1)
    lh_s = jnp.stack(outs_s[3], axis=1)
    lc_p = jnp.stack(outs_p[4], axis=1)
    lc_s = jnp.stack(outs_s[4], axis=1)
    fc_p = jnp.stack(outs_p[5], axis=1)
    fc_s = jnp.stack(outs_s[5], axis=1)
    return (xp, xs[:, :n_tok], k_p, v_p, k_s, v_s, hg_p, hg_s, lh_p, lh_s, lc_p, lc_s, fc_p, fc_s)
```
